```python
import jax, jax.numpy as jnp
from jax import lax
import numpy as np

D_MODEL = 4096
BATCH = 2
SEQ = 8192
DEPTH = 2

HEAD_DIM = 128
A_HEADS = 12
A_BRANCHES = ((128, 1), (512, 4), (2048, 16))
A_BLK = 128
NSA_HEADS = 12
NSA_KV_HEADS = 3
NSA_GROUP = NSA_HEADS // NSA_KV_HEADS
CMP_LEN = 32
CMP_STRIDE = 16
SEL_BLK = 64
TOP_N = 16
NSA_WIN = 512
NSA_QB = 64
CONV_CH = D_MODEL - (A_HEADS + NSA_HEADS) * HEAD_DIM
CONV_W = 3
D_FF = 11008
N_EXPERTS = 8
TOP_K = 2
D_FF_EXPERT = 3584
ALPHA = (2 * DEPTH) ** 0.25
BETA = (8 * DEPTH) ** -0.25
LN_EPS = 1e-5
NEG = -1e30
TINY = 1e-30
FORCE = 1e4
A_W = A_HEADS * HEAD_DIM
N_W = NSA_HEADS * HEAD_DIM
KV_W = NSA_KV_HEADS * HEAD_DIM
SPLITS = [A_W, A_W, A_W, N_W, KV_W, KV_W, KV_W, KV_W, KV_W, KV_W, NSA_HEADS * 3, CONV_CH, CONV_CH, CONV_CH]
N_IN = sum(SPLITS)
N_DENSE = (DEPTH + 1) // 2
N_MOE = DEPTH // 2

kernel_name = "hybrid_dilated_nsa_shortconv_moe_block"


def layer_norm(x, g=None, b=None):
    xf = x.astype(jnp.float32)
    mu = jnp.mean(xf, axis=-1, keepdims=True)
    var = jnp.mean(jnp.square(xf - mu), axis=-1, keepdims=True)
    y = (xf - mu) * lax.rsqrt(var + LN_EPS)
    if g is not None:
        y = y * g.astype(jnp.float32) + b.astype(jnp.float32)
    return y.astype(x.dtype)


def rms_norm(x, g):
    xf = x.astype(jnp.float32)
    y = xf * lax.rsqrt(jnp.mean(jnp.square(xf), axis=-1, keepdims=True) + LN_EPS)
    return (y * g.astype(jnp.float32)).astype(x.dtype)


def masked_softmax(s, mask):
    s = jnp.where(mask, s, NEG)
    m = jnp.max(s, axis=-1, keepdims=True)
    e = jnp.where(mask, jnp.exp(s - m), 0.0)
    den = jnp.sum(e, axis=-1, keepdims=True)
    p = e / jnp.maximum(den, TINY)
    lse = (m + jnp.log(jnp.maximum(den, TINY)))[..., 0]
    return p, lse


def alibi_slopes():
    n = A_HEADS + NSA_HEADS
    sl = jnp.exp2(-8.0 * jnp.arange(1, n + 1, dtype=jnp.float32) / n)
    return sl[0::2], sl[1::2]


def dilated_branch(q, k, v, window, dil, slopes):
    steps = window // dil
    Bn, H, S, hd = q.shape
    span = dil * A_BLK
    P = -(-S // span) * span
    L = P // dil
    nb = L // A_BLK

    def fold(t):
        t = jnp.pad(t, ((0, 0), (0, 0), (0, P - S), (0, 0)))
        return t.reshape(Bn, H, L, dil, hd).transpose(0, 1, 3, 2, 4).reshape(Bn, H, dil, nb, A_BLK, hd)

    def band(t):
        tp = jnp.pad(t, ((0, 0), (0, 0), (0, 0), (1, 0), (0, 0), (0, 0)))
        return jnp.concatenate([tp[:, :, :, :-1], tp[:, :, :, 1:]], axis=4)

    qb = fold(q)
    kw = band(fold(k))
    vw = band(fold(v))
    s = jnp.einsum('bhrnqd,bhrnkd->bhrnqk', qb, kw).astype(jnp.float32)
    qi = jnp.arange(A_BLK)[:, None]
    kj = jnp.arange(2 * A_BLK)[None, :]
    j = A_BLK + qi - kj
    blk = jnp.arange(nb)[:, None, None]
    mask = (j >= 0) & (j <= steps) & ((blk > 0) | (kj >= A_BLK))
    s = s - slopes[None, :, None, None, None, None] * (j * dil).astype(jnp.float32)
    p, lse = masked_softmax(s, mask)
    o = jnp.einsum('bhrnqk,bhrnkd->bhrnqd', p.astype(v.dtype), vw)
    o = o.reshape(Bn, H, dil, L, hd).transpose(0, 1, 3, 2, 4).reshape(Bn, H, P, hd)[:, :, :S]
    lse = lse.reshape(Bn, H, dil, L).transpose(0, 1, 3, 2).reshape(Bn, H, P)[:, :, :S]
    return o, lse


def dilated_mixer(q, k, v, slopes):
    outs, lses = [], []
    for window, dil in A_BRANCHES:
        o, l = dilated_branch(q, k, v, window, dil, slopes)
        outs.append(o)
        lses.append(l)
    w = jax.nn.softmax(jnp.stack(lses, 0), axis=0)
    o = jnp.einsum('rbhs,rbhsd->bhsd', w, jnp.stack(outs, 0).astype(jnp.float32))
    return o.astype(q.dtype)


def nsa_mixer(q, kc, vc, ks, vs, kw, vw, gates, slopes, cmp_wk, cmp_wv, cmp_pe):
    Bn, S, _ = q.shape
    G, M, hd = NSA_KV_HEADS, NSA_GROUP, HEAD_DIM
    qg = q.reshape(Bn, S, G, M, hd).transpose(0, 2, 3, 1, 4)
    gt = jax.nn.sigmoid(gates.astype(jnp.float32)).reshape(Bn, S, G, M, 3).transpose(0, 2, 3, 1, 4)

    def kvh(t):
        return t.reshape(Bn, S, G, hd).transpose(0, 2, 1, 3)

    n_cmp = S // CMP_STRIDE - 1

    def compress(t, w):
        ch = kvh(t).reshape(Bn, G, S // CMP_STRIDE, CMP_STRIDE, hd)
        blocks = jnp.concatenate([ch[:, :, :-1], ch[:, :, 1:]], axis=3) + cmp_pe
        return blocks.reshape(Bn, G, n_cmp, CMP_LEN * hd) @ w

    k_cmp = compress(kc, cmp_wk)
    v_cmp = compress(vc, cmp_wv)
    c_start = jnp.arange(n_cmp) * CMP_STRIDE
    c_end = c_start + CMP_LEN - 1
    n_blk = S // SEL_BLK
    n_sel = min(TOP_N, n_blk)
    s_start = jnp.arange(n_blk) * SEL_BLK
    overlap = ((c_start[:, None] < s_start[None, :] + SEL_BLK) &
               (c_end[:, None] >= s_start[None, :])).astype(jnp.float32)
    k_sel_blocks = kvh(ks).reshape(Bn, G, n_blk, SEL_BLK, hd)
    v_sel_blocks = kvh(vs).reshape(Bn, G, n_blk, SEL_BLK, hd)
    pad = ((0, 0), (0, 0), (NSA_WIN, 0), (0, 0))
    k_win = jnp.pad(kvh(kw), pad)
    v_win = jnp.pad(kvh(vw), pad)
    sl = slopes.reshape(G, M)[None, :, :, None, None]
    bi = jnp.arange(Bn)[:, None, None, None]
    gi = jnp.arange(G)[None, :, None, None]

    def block(q0):
        qb = lax.dynamic_slice_in_dim(qg, q0, NSA_QB, axis=3)
        gb = lax.dynamic_slice_in_dim(gt, q0, NSA_QB, axis=3)
        t = q0 + jnp.arange(NSA_QB)
        dist = t[:, None] - c_end[None, :]
        s = jnp.einsum('bgmqd,bgcd->bgmqc', qb, k_cmp).astype(jnp.float32) - sl * dist
        p_cmp, _ = masked_softmax(s, dist >= 0)
        o_cmp = jnp.einsum('bgmqc,bgcd->bgmqd', p_cmp.astype(v_cmp.dtype), v_cmp)
        imp = jnp.einsum('bgmqc,cn->bgqn', p_cmp, overlap)
        cur = t[:, None] // SEL_BLK
        jb = jnp.arange(n_blk)[None, :]
        forced = (jb == 0) | (jb == cur) | (jb == cur - 1)
        imp = jnp.where(forced, imp + FORCE, imp)
        imp = jnp.where(jb * SEL_BLK <= t[:, None], imp, NEG)
        top_val, top_idx = lax.top_k(imp, n_sel)
        k_sel = k_sel_blocks[bi, gi, top_idx].reshape(Bn, G, NSA_QB, n_sel * SEL_BLK, hd)
        v_sel = v_sel_blocks[bi, gi, top_idx].reshape(Bn, G, NSA_QB, n_sel * SEL_BLK, hd)
        pos = (top_idx[..., None] * SEL_BLK + jnp.arange(SEL_BLK)).reshape(Bn, G, NSA_QB, n_sel * SEL_BLK)
        rel = t[None, None, :, None] - pos
        sel_ok = jnp.broadcast_to((top_val > 0.5 * NEG)[..., None],
                                  (Bn, G, NSA_QB, n_sel, SEL_BLK)).reshape(Bn, G, NSA_QB, n_sel * SEL_BLK)
        ok = sel_ok & (rel >= 0)
        s = jnp.einsum('bgmqd,bgqkd->bgmqk', qb, k_sel).astype(jnp.float32) - sl * rel[:, :, None]
        p, _ = masked_softmax(s, ok[:, :, None])
        o_sel = jnp.einsum('bgmqk,bgqkd->bgmqd', p.astype(v_sel.dtype), v_sel)
        kb = lax.dynamic_slice_in_dim(k_win, q0, NSA_QB + NSA_WIN, axis=2)
        vb = lax.dynamic_slice_in_dim(v_win, q0, NSA_QB + NSA_WIN, axis=2)
        kpos = q0 - NSA_WIN + jnp.arange(NSA_QB + NSA_WIN)
        relw = t[:, None] - kpos[None, :]
        okw = (relw >= 0) & (relw < NSA_WIN) & (kpos[None, :] >= 0)
        s = jnp.einsum('bgmqd,bgkd->bgmqk', qb, kb).astype(jnp.float32) - sl * relw
        p, _ = masked_softmax(s, okw)
        o_win = jnp.einsum('bgmqk,bgkd->bgmqd', p.astype(vb.dtype), vb)
        o = gb[..., 0:1] * o_cmp + gb[..., 1:2] * o_sel + gb[..., 2:3] * o_win
        return o.astype(q.dtype)

    outs = lax.map(block, jnp.arange(S // NSA_QB) * NSA_QB)
    return outs.transpose(1, 0, 4, 2, 3, 5).reshape(Bn, S, N_W)


def short_conv(u, bg, cg, conv_w):
    z = cg * u
    y = lax.conv_general_dilated(z, conv_w[:, None, :].astype(z.dtype), (1,), [(CONV_W - 1, 0)],
                                 dimension_numbers=('NWC', 'WIO', 'NWC'), feature_group_count=CONV_CH)
    return bg * y


def token_mixing(h, w_in, conv_w, cmp_wk, cmp_wv, cmp_pe, mix_g, w_out):
    Bn, S, _ = h.shape
    proj = h @ w_in
    offs = np.cumsum(SPLITS)[:-1].tolist()
    qa, ka, va, qn, kc, vc, ks, vs, kw, vw, gn, u, bg, cg = jnp.split(proj, offs, axis=-1)
    scale = HEAD_DIM ** -0.5
    sl_a, sl_n = alibi_slopes()

    def heads(t):
        return t.reshape(Bn, S, A_HEADS, HEAD_DIM).transpose(0, 2, 1, 3)

    oa = dilated_mixer(heads(qa) * scale, heads(ka), heads(va), sl_a)
    oa = oa.transpose(0, 2, 1, 3).reshape(Bn, S, A_W)
    on = nsa_mixer(qn * scale, kc, vc, ks, vs, kw, vw, gn, sl_n, cmp_wk, cmp_wv, cmp_pe)
    oc = short_conv(u, bg, cg, conv_w)
    y = jnp.concatenate([rms_norm(oa, mix_g[:A_W]),
                         rms_norm(on, mix_g[A_W:A_W + N_W]),
                         rms_norm(oc, mix_g[A_W + N_W:])], axis=-1)
    return y @ w_out


def swiglu(h, wg, wu, wd):
    return (jax.nn.silu(h @ wg) * (h @ wu)) @ wd


def moe(h, router, router_b, wg, wu, wd):
    Bn, S, D = h.shape
    hf = h.reshape(Bn * S, D)
    logits = (hf @ router).astype(jnp.float32) + router_b.astype(jnp.float32)
    top_val, top_idx = lax.top_k(logits, TOP_K)
    w = jax.nn.softmax(top_val, axis=-1)
    gate = jnp.sum(jax.nn.one_hot(top_idx, N_EXPERTS, dtype=jnp.float32) * w[..., None], axis=1)
    y = jnp.zeros_like(hf)
    for e in range(N_EXPERTS):
        y = y + (gate[:, e:e + 1] * swiglu(hf, wg[e], wu[e], wd[e])).astype(hf.dtype)
    return y.reshape(Bn, S, D)


def setup_inputs(seed: int = 0) -> dict:
    key = jax.random.key(seed)
    ks = jax.random.split(key, 23)
    D = D_MODEL

    def nrm(k, shape, s):
        return s * jax.random.normal(k, shape, jnp.float32)

    return {
        "x": nrm(ks[0], (BATCH, SEQ, D), 1.0),
        "c": nrm(ks[1], (BATCH, D), 1.0),
        "ada_w": nrm(ks[2], (DEPTH, D, 6 * D), 0.1 * D ** -0.5),
        "ada_b": nrm(ks[3], (DEPTH, 6 * D), 0.01),
        "w_in": nrm(ks[4], (DEPTH, D, N_IN), D ** -0.5),
        "conv_w": nrm(ks[5], (DEPTH, CONV_W, CONV_CH), CONV_W ** -0.5),
        "cmp_wk": nrm(ks[6], (DEPTH, CMP_LEN * HEAD_DIM, HEAD_DIM), (CMP_LEN * HEAD_DIM) ** -0.5),
        "cmp_wv": nrm(ks[7], (DEPTH, CMP_LEN * HEAD_DIM, HEAD_DIM), (CMP_LEN * HEAD_DIM) ** -0.5),
        "cmp_pe": nrm(ks[8], (DEPTH, CMP_LEN, HEAD_DIM), 0.1),
        "mix_g": 1.0 + nrm(ks[9], (DEPTH, D), 0.05),
        "w_out": nrm(ks[10], (DEPTH, D, D), BETA * D ** -0.5),
        "ln1_g": 1.0 + nrm(ks[11], (DEPTH, D), 0.05),
        "ln1_b": nrm(ks[12], (DEPTH, D), 0.02),
        "ln2_g": 1.0 + nrm(ks[13], (DEPTH, D), 0.05),
        "ln2_b": nrm(ks[14], (DEPTH, D), 0.02),
        "ffn_w_gate": nrm(ks[15], (N_DENSE, D, D_FF), D ** -0.5),
        "ffn_w_up": nrm(ks[16], (N_DENSE, D, D_FF), D ** -0.5),
        "ffn_w_down": nrm(ks[17], (N_DENSE, D_FF, D), BETA * D_FF ** -0.5),
        "moe_router": nrm(ks[18], (N_MOE, D, N_EXPERTS), D ** -0.5),
        "moe_router_b": nrm(ks[19], (N_MOE, N_EXPERTS), 0.01),
        "moe_w_gate": nrm(ks[20], (N_MOE, N_EXPERTS, D, D_FF_EXPERT), D ** -0.5),
        "moe_w_up": nrm(ks[21], (N_MOE, N_EXPERTS, D, D_FF_EXPERT), D ** -0.5),
        "moe_w_down": nrm(ks[22], (N_MOE, N_EXPERTS, D_FF_EXPERT, D), BETA * D_FF_EXPERT ** -0.5),
    }


def reference(x, c, ada_w, ada_b, w_in, conv_w, cmp_wk, cmp_wv, cmp_pe, mix_g, w_out,
              ln1_g, ln1_b, ln2_g, ln2_b, ffn_w_gate, ffn_w_up, ffn_w_down,
              moe_router, moe_router_b, moe_w_gate, moe_w_up, moe_w_down):
    cond = jax.nn.silu(c)
    for i in range(DEPTH):
        mod = cond @ ada_w[i] + ada_b[i]
        sh1, sc1, g1, sh2, sc2, g2 = [m[:, None, :] for m in jnp.split(mod, 6, axis=-1)]
        h = layer_norm(x) * (1.0 + sc1) + sh1
        y = token_mixing(h, w_in[i], conv_w[i], cmp_wk[i], cmp_wv[i], cmp_pe[i], mix_g[i], w_out[i])
        x = layer_norm(ALPHA * x + (1.0 + g1) * y, ln1_g[i], ln1_b[i])
        h = layer_norm(x) * (1.0 + sc2) + sh2
        if i % 2 == 0:
            j = i // 2
            y = swiglu(h, ffn_w_gate[j], ffn_w_up[j], ffn_w_down[j])
        else:
            j = i // 2
            y = moe(h, moe_router[j], moe_router_b[j], moe_w_gate[j], moe_w_up[j], moe_w_down[j])
        x = layer_norm(ALPHA * x + (1.0 + g2) * y, ln2_g[i], ln2_b[i])
    return x
```

```python
import functools

import jax
import jax.numpy as jnp
from jax import lax
from jax.experimental import pallas as pl
from jax.experimental.pallas import tpu as pltpu

F32 = jnp.float32
BF16 = jnp.bfloat16

HEAD_DIM = 128
A_HEADS = 12
A_BRANCHES = ((128, 1), (512, 4), (2048, 16))
A_BLK = 128
A_SUPER = A_BLK * max(d for _, d in A_BRANCHES)
NSA_HEADS = 12
NSA_KV_HEADS = 3
NSA_GROUP = NSA_HEADS // NSA_KV_HEADS
CMP_LEN = 32
CMP_STRIDE = 16
SEL_BLK = 64
TOP_N = 16
NSA_WIN = 512
CONV_W = 3
N_EXPERTS = 8
LN_EPS = 1e-5
NEG = -1e30
TINY = 1e-30
FORCE = 1e4
LOWEST = -3e38

A_W = A_HEADS * HEAD_DIM
N_W = NSA_HEADS * HEAD_DIM
KV_W = NSA_KV_HEADS * HEAD_DIM
ATT_W = 3 * A_W + N_W + 6 * KV_W
GATE_W = NSA_HEADS * 3
LANES = 128
VMEM_LIMIT = 56 * 1024 * 1024


def _cparams(sem):
    return pltpu.CompilerParams(dimension_semantics=sem, vmem_limit_bytes=VMEM_LIMIT)


def _pick(n, pref):
    if n <= pref:
        return n
    t = pref
    while n % t:
        t //= 2
    return t


def _mm_kernel(*refs, nk, has_scale):
    if has_scale:
        x_ref, w_ref, s_ref, o_ref, acc_ref = refs
    else:
        x_ref, w_ref, o_ref, acc_ref = refs
        s_ref = None
    part = jnp.dot(x_ref[...], w_ref[...].astype(BF16), preferred_element_type=F32)

    def finish(acc):
        if s_ref is not None:
            acc = acc * s_ref[...]
        o_ref[...] = acc.astype(o_ref.dtype)

    if nk == 1:
        finish(part)
        return
    k = pl.program_id(2)

    @pl.when(k == 0)
    def _():
        acc_ref[...] = part

    @pl.when(k > 0)
    def _():
        acc_ref[...] += part

    @pl.when(k == nk - 1)
    def _():
        finish(acc_ref[...])


def matmul(x, w, w_idx, n_cols, *, col0=0, scale=None, out_dtype=F32, tm=1024, tn=512, tk=None):
    M, K = x.shape
    tm = _pick(M, tm)
    tn = _pick(n_cols, tn)
    tk = K if tk is None else tk
    assert col0 % tn == 0 and K % tk == 0
    nk = K // tk
    cb = col0 // tn
    lead = (w_idx,) if w.ndim == 3 else ()
    wblk = ((None,) if w.ndim == 3 else ()) + (tk, tn)
    in_specs = [pl.BlockSpec((tm, tk), lambda i, j, k: (i, k)),
                pl.BlockSpec(wblk, lambda i, j, k: lead + (k, cb + j))]
    args = [x, w]
    if scale is not None:
        in_specs.append(pl.BlockSpec((1, tn), lambda i, j, k: (0, j)))
        args.append(scale)
    return pl.pallas_call(
        functools.partial(_mm_kernel, nk=nk, has_scale=scale is not None),
        out_shape=jax.ShapeDtypeStruct((M, n_cols), out_dtype),
        grid=(M // tm, n_cols // tn, nk),
        in_specs=in_specs,
        out_specs=pl.BlockSpec((tm, tn), lambda i, j, k: (i, j)),
        scratch_shapes=[pltpu.VMEM((tm, tn), F32)],
        compiler_params=_cparams(("parallel", "parallel", "arbitrary")),
        name="matmul",
    )(*args)


def _gate_up_kernel(x_ref, wg_ref, wu_ref, o_ref):
    x = x_ref[...]
    g = jnp.dot(x, wg_ref[...].astype(BF16), preferred_element_type=F32)
    u = jnp.dot(x, wu_ref[...].astype(BF16), preferred_element_type=F32)
    o_ref[...] = (g * (1.0 / (1.0 + jnp.exp(-g))) * u).astype(o_ref.dtype)


def gate_up(x, wg, wu, lead, *, tm=1024, tn=256):
    M, K = x.shape
    F = wg.shape[-1]
    tm = _pick(M, tm)
    tn = _pick(F, tn)
    nlead = len(lead)
    wspec = pl.BlockSpec((None,) * nlead + (K, tn), lambda i, j: lead + (0, j))
    return pl.pallas_call(
        _gate_up_kernel,
        out_shape=jax.ShapeDtypeStruct((M, F), BF16),
        grid=(M // tm, F // tn),
        in_specs=[pl.BlockSpec((tm, K), lambda i, j: (i, 0)), wspec, wspec],
        out_specs=pl.BlockSpec((tm, tn), lambda i, j: (i, j)),
        compiler_params=_cparams(("parallel", "parallel")),
        name="gate_up",
    )(x, wg, wu)


def _down_kernel(*refs, nk, expert):
    if expert is None:
        a_ref, w_ref, o_ref, acc_ref = refs
        g_ref = y_ref = None
    else:
        a_ref, w_ref, g_ref, y_ref, o_ref, acc_ref = refs
    part = jnp.dot(a_ref[...], w_ref[...].astype(BF16), preferred_element_type=F32)

    def finish(acc):
        if expert is None:
            o_ref[...] = acc
        else:
            o_ref[...] = y_ref[...] + g_ref[:, expert:expert + 1] * acc

    if nk == 1:
        finish(part)
        return
    k = pl.program_id(2)

    @pl.when(k == 0)
    def _():
        acc_ref[...] = part

    @pl.when(k > 0)
    def _():
        acc_ref[...] += part

    @pl.when(k == nk - 1)
    def _():
        finish(acc_ref[...])


def down(a, wd, lead, *, gate=None, y_prev=None, expert=None, tm=1024, tn=512, nk=1):
    M, F = a.shape
    D = wd.shape[-1]
    tm = _pick(M, tm)
    tn = _pick(D, tn)
    tk = F // nk
    nlead = len(lead)
    in_specs = [pl.BlockSpec((tm, tk), lambda i, j, k: (i, k)),
                pl.BlockSpec((None,) * nlead + (tk, tn), lambda i, j, k: lead + (k, j))]
    args = [a, wd]
    aliases = {}
    if expert is not None:
        in_specs += [pl.BlockSpec((tm, LANES), lambda i, j, k: (i, 0)),
                     pl.BlockSpec((tm, tn), lambda i, j, k: (i, j))]
        args += [gate, y_prev]
        aliases = {3: 0}
    return pl.pallas_call(
        functools.partial(_down_kernel, nk=nk, expert=expert),
        out_shape=jax.ShapeDtypeStruct((M, D), F32),
        grid=(M // tm, D // tn, nk),
        in_specs=in_specs,
        out_specs=pl.BlockSpec((tm, tn), lambda i, j, k: (i, j)),
        scratch_shapes=[pltpu.VMEM((tm, tn), F32)],
        input_output_aliases=aliases,
        compiler_params=_cparams(("parallel", "parallel", "arbitrary")),
        name="down",
    )(*args)


def _ln(v):
    mu = jnp.mean(v, axis=-1, keepdims=True)
    c = v - mu
    var = jnp.mean(c * c, axis=-1, keepdims=True)
    return c * lax.rsqrt(var + LN_EPS)


def _ln_mod_kernel(x_ref, sc_ref, sh_ref, h_ref):
    h_ref[0] = (_ln(x_ref[0]) * (1.0 + sc_ref[0]) + sh_ref[0]).astype(h_ref.dtype)


def ln_mod(x, sc, sh, *, tm=256):
    B, S, D = x.shape
    tm = _pick(S, tm)
    row = pl.BlockSpec((1, tm, D), lambda b, i: (b, i, 0))
    vec = pl.BlockSpec((1, 1, D), lambda b, i: (b, 0, 0))
    return pl.pallas_call(
        _ln_mod_kernel,
        out_shape=jax.ShapeDtypeStruct((B, S, D), BF16),
        grid=(B, S // tm),
        in_specs=[row, vec, vec],
        out_specs=row,
        compiler_params=_cparams(("parallel", "parallel")),
        name="ln_mod",
    )(x, sc, sh)


def _split3(v):
    hi = v.astype(BF16)
    r1 = v - hi.astype(F32)
    mid = r1.astype(BF16)
    lo = (r1 - mid.astype(F32)).astype(BF16)
    return hi, mid, lo


def _res_ln_kernel(*refs, alpha, with_h, with_router):
    it = iter(refs)
    x_ref, y_ref, g_ref, lng_ref, lnb_ref = (next(it) for _ in range(5))
    sc_ref = sh_ref = rw_ref = rb_ref = None
    if with_h:
        sc_ref, sh_ref = next(it), next(it)
    if with_router:
        rw_ref, rb_ref = next(it), next(it)
    xo_ref = next(it)
    h_ref = next(it) if with_h else None
    gate_ref = next(it) if with_router else None

    v = alpha * x_ref[0] + (1.0 + g_ref[0]) * y_ref[0]
    xn = _ln(v) * lng_ref[...] + lnb_ref[...]
    xo_ref[0] = xn
    if not with_h:
        return
    h = _ln(xn) * (1.0 + sc_ref[0]) + sh_ref[0]
    h_ref[0] = h.astype(h_ref.dtype)
    if not with_router:
        return
    hs = _split3(h)
    ws = _split3(rw_ref[...])
    logits = rb_ref[...]
    for a in range(3):
        for b in range(3 - a):
            logits = logits + jnp.dot(hs[a], ws[b], preferred_element_type=F32)
    lane = lax.broadcasted_iota(jnp.int32, logits.shape, 1).astype(F32)
    logits = jnp.where(lane < N_EXPERTS, logits, LOWEST)
    v1 = jnp.max(logits, axis=-1, keepdims=True)
    i1 = jnp.min(jnp.where(logits == v1, lane, float(LANES)), axis=-1, keepdims=True)
    rest = jnp.where(lane == i1, LOWEST, logits)
    v2 = jnp.max(rest, axis=-1, keepdims=True)
    i2 = jnp.min(jnp.where(rest == v2, lane, float(LANES)), axis=-1, keepdims=True)
    e2 = jnp.exp(v2 - v1)
    den = 1.0 + e2
    gate_ref[0] = jnp.where(lane == i1, 1.0 / den, jnp.where(lane == i2, e2 / den, 0.0))


def res_ln(x, y, g, lng, lnb, alpha, *, sc=None, sh=None, router_w=None, router_b=None, tm=256):
    B, S, D = x.shape
    tm = _pick(S, tm)
    with_h = sc is not None
    with_router = router_w is not None
    row = pl.BlockSpec((1, tm, D), lambda b, i: (b, i, 0))
    vec = pl.BlockSpec((1, 1, D), lambda b, i: (b, 0, 0))
    par = pl.BlockSpec((1, D), lambda b, i: (0, 0))
    in_specs = [row, row, vec, par, par]
    args = [x, y, g, lng, lnb]
    out_shape = [jax.ShapeDtypeStruct((B, S, D), F32)]
    out_specs = [row]
    if with_h:
        in_specs += [vec, vec]
        args += [sc, sh]
        out_shape.append(jax.ShapeDtypeStruct((B, S, D), BF16))
        out_specs.append(row)
    if with_router:
        in_specs += [pl.BlockSpec((D, LANES), lambda b, i: (0, 0)), pl.BlockSpec((1, LANES), lambda b, i: (0, 0))]
        args += [router_w, router_b]
        out_shape.append(jax.ShapeDtypeStruct((B, S, LANES), F32))
        out_specs.append(pl.BlockSpec((1, tm, LANES), lambda b, i: (b, i, 0)))
    return pl.pallas_call(
        functools.partial(_res_ln_kernel, alpha=alpha, with_h=with_h, with_router=with_router),
        out_shape=out_shape,
        grid=(B, S // tm),
        in_specs=in_specs,
        out_specs=out_specs,
        compiler_params=_cparams(("parallel", "parallel")),
        name="res_ln",
    )(*args)


def _rms(v, g):
    return (v * lax.rsqrt(jnp.mean(v * v, axis=-1, keepdims=True) + LN_EPS) * g).astype(BF16)


def _out_proj_kernel(oa_ref, on_ref, oc_ref, ga_ref, gn_ref, gc_ref, wa_ref, wn_ref, wc_ref, o_ref,
                     ya_sc, yn_sc, yc_sc):
    @pl.when(pl.program_id(1) == 0)
    def _():
        ya_sc[...] = _rms(oa_ref[...], ga_ref[...])
        yn_sc[...] = _rms(on_ref[...], gn_ref[...])
        yc_sc[...] = _rms(oc_ref[...], gc_ref[...])

    acc = jnp.dot(ya_sc[...], wa_ref[...].astype(BF16), preferred_element_type=F32)
    acc += jnp.dot(yn_sc[...], wn_ref[...].astype(BF16), preferred_element_type=F32)
    acc += jnp.dot(yc_sc[...], wc_ref[...].astype(BF16), preferred_element_type=F32)
    o_ref[...] = acc


def out_proj(oa, on, oc, mix_g, w_out, layer, *, tm=512, tn=512):
    M = oa.shape[0]
    C = oc.shape[1]
    D = w_out.shape[-1]
    tm = _pick(M, tm)
    tn = _pick(D, tn)
    assert A_W == N_W and (A_W + N_W) % C == 0
    ga = mix_g[layer:layer + 1, :A_W]
    gn = mix_g[layer:layer + 1, A_W:A_W + N_W]
    gc = mix_g[layer:layer + 1, A_W + N_W:]
    return pl.pallas_call(
        _out_proj_kernel,
        out_shape=jax.ShapeDtypeStruct((M, D), F32),
        grid=(M // tm, D // tn),
        in_specs=[pl.BlockSpec((tm, A_W), lambda i, j: (i, 0)),
                  pl.BlockSpec((tm, N_W), lambda i, j: (i, 0)),
                  pl.BlockSpec((tm, C), lambda i, j: (i, 0)),
                  pl.BlockSpec((1, A_W), lambda i, j: (0, 0)),
                  pl.BlockSpec((1, N_W), lambda i, j: (0, 0)),
                  pl.BlockSpec((1, C), lambda i, j: (0, 0)),
                  pl.BlockSpec((None, A_W, tn), lambda i, j: (layer, 0, j)),
                  pl.BlockSpec((None, N_W, tn), lambda i, j: (layer, 1, j)),
                  pl.BlockSpec((None, C, tn), lambda i, j: (layer, (A_W + N_W) // C, j))],
        out_specs=pl.BlockSpec((tm, tn), lambda i, j: (i, j)),
        scratch_shapes=[pltpu.VMEM((tm, A_W), BF16), pltpu.VMEM((tm, N_W), BF16), pltpu.VMEM((tm, C), BF16)],
        compiler_params=_cparams(("parallel", "arbitrary")),
        name="out_proj",
    )(oa, on, oc, ga, gn, gc, w_out, w_out, w_out)


def _conv_kernel(cur_ref, prev_ref, w_ref, o_ref, *, C):
    i = pl.program_id(1)
    cur = cur_ref[0]
    u, bg, cg = (cur[:, LANES + k * C:LANES + (k + 1) * C] for k in range(3))
    z = cg * u
    prev = prev_ref[0]
    zp = prev[:, LANES + 2 * C:LANES + 3 * C] * prev[:, LANES:LANES + C]
    zp = jnp.where(i > 0, zp, 0.0)
    rows = lax.broadcasted_iota(jnp.int32, z.shape, 0)
    z1 = jnp.where(rows >= 1, pltpu.roll(z, 1, axis=0), zp[7:8, :])
    z2 = jnp.where(rows >= 2, pltpu.roll(z, 2, axis=0), jnp.where(rows == 1, zp[7:8, :], zp[6:7, :]))
    w = w_ref[...]
    o_ref[0] = bg * (w[0:1, :] * z2 + w[1:2, :] * z1 + w[2:3, :] * z)


def short_conv(pb, conv_w_l, *, tm=512):
    B, S, W = pb.shape
    C = (W - LANES) // 3
    tm = _pick(S, tm)
    return pl.pallas_call(
        functools.partial(_conv_kernel, C=C),
        out_shape=jax.ShapeDtypeStruct((B, S, C), F32),
        grid=(B, S // tm),
        in_specs=[pl.BlockSpec((1, tm, W), lambda b, i: (b, i, 0)),
                  pl.BlockSpec((1, 8, W), lambda b, i: (b, jnp.maximum(i * (tm // 8) - 1, 0), 0)),
                  pl.BlockSpec((CONV_W, C), lambda b, i: (0, 0))],
        out_specs=pl.BlockSpec((1, tm, C), lambda b, i: (b, i, 0)),
        compiler_params=_cparams(("parallel", "parallel")),
        name="short_conv",
    )(pb, pb, conv_w_l)


def _dilated_kernel(sl_ref, q_ref, k_ref, v_ref, o_ref, qf, kf, vf, u_sc, m_sc, l_sc):
    h = pl.program_id(1)
    sb = pl.program_id(2)
    base = sb * A_SUPER

    @pl.when(sb == 0)
    def _():
        kf[...] = k_ref[0].astype(F32)
        vf[...] = v_ref[0].astype(F32)

    qf[...] = q_ref[0].astype(F32)
    slope = sl_ref[h]
    qi = lax.broadcasted_iota(jnp.int32, (A_BLK, A_BLK), 0)
    kj = lax.broadcasted_iota(jnp.int32, (A_BLK, A_BLK), 1)
    diff = (qi - kj).astype(F32)
    cur_ok = kj <= qi
    prev_band = kj >= qi
    nt = (((1,), (1,)), ((), ()))

    for br, (_, d) in enumerate(A_BRANCHES):
        span = A_BLK * d
        sd = slope * float(d)
        bias_c = sd * diff
        bias_p = sd * (diff + float(A_BLK))

        def step(i, carry, d=d, span=span, br=br, bias_c=bias_c, bias_p=bias_p):
            loc = (i // d) * span + (i % d)
            start = base + loc
            has_prev = start >= span
            pstart = jnp.where(has_prev, start - span, start)
            rows = pl.ds(start, A_BLK, stride=d) if d > 1 else pl.ds(start, A_BLK)
            prow = pl.ds(pstart, A_BLK, stride=d) if d > 1 else pl.ds(pstart, A_BLK)
            lrow = pl.ds(loc, A_BLK, stride=d) if d > 1 else pl.ds(loc, A_BLK)
            q = qf[lrow, :].astype(BF16)
            kc = kf[rows, :].astype(BF16)
            kp = kf[prow, :].astype(BF16)
            s_c = lax.dot_general(q, kc, nt, preferred_element_type=F32) - bias_c
            s_p = lax.dot_general(q, kp, nt, preferred_element_type=F32) - bias_p
            ok_p = prev_band & has_prev
            s_c = jnp.where(cur_ok, s_c, NEG)
            s_p = jnp.where(ok_p, s_p, NEG)
            m = jnp.maximum(jnp.max(s_c, axis=-1, keepdims=True), jnp.max(s_p, axis=-1, keepdims=True))
            e_c = jnp.where(cur_ok, jnp.exp(s_c - m), 0.0)
            e_p = jnp.where(ok_p, jnp.exp(s_p - m), 0.0)
            l = jnp.sum(e_c, axis=-1, keepdims=True) + jnp.sum(e_p, axis=-1, keepdims=True)
            u = jnp.dot(e_c.astype(BF16), vf[rows, :].astype(BF16), preferred_element_type=F32)
            u += jnp.dot(e_p.astype(BF16), vf[prow, :].astype(BF16), preferred_element_type=F32)
            u_sc[br, lrow, :] = u
            m_sc[br, lrow, :] = jnp.broadcast_to(m, (A_BLK, HEAD_DIM))
            l_sc[br, lrow, :] = jnp.broadcast_to(l, (A_BLK, HEAD_DIM))
            return carry

        lax.fori_loop(0, A_SUPER // A_BLK, step, 0)

    nb = len(A_BRANCHES)
    mt = m_sc[0]
    for br in range(1, nb):
        mt = jnp.maximum(mt, m_sc[br])
    num = jnp.zeros((A_SUPER, HEAD_DIM), F32)
    den = jnp.zeros((A_SUPER, HEAD_DIM), F32)
    for br in range(nb):
        w = jnp.exp(m_sc[br] - mt)
        num += w * u_sc[br]
        den += w * l_sc[br]
    o_ref[0] = num / den


def dilated_attention(pa, slopes):
    B, S, _ = pa.shape
    assert S % A_SUPER == 0
    grid_spec = pltpu.PrefetchScalarGridSpec(
        num_scalar_prefetch=1,
        grid=(B, A_HEADS, S // A_SUPER),
        in_specs=[pl.BlockSpec((1, A_SUPER, HEAD_DIM), lambda b, h, s, sl: (b, s, h)),
                  pl.BlockSpec((1, S, HEAD_DIM), lambda b, h, s, sl: (b, 0, A_HEADS + h)),
                  pl.BlockSpec((1, S, HEAD_DIM), lambda b, h, s, sl: (b, 0, 2 * A_HEADS + h))],
        out_specs=pl.BlockSpec((1, A_SUPER, HEAD_DIM), lambda b, h, s, sl: (b, s, h)),
        scratch_shapes=[pltpu.VMEM((A_SUPER, HEAD_DIM), F32),
                        pltpu.VMEM((S, HEAD_DIM), F32),
                        pltpu.VMEM((S, HEAD_DIM), F32),
                        pltpu.VMEM((len(A_BRANCHES), A_SUPER, HEAD_DIM), F32),
                        pltpu.VMEM((len(A_BRANCHES), A_SUPER, HEAD_DIM), F32),
                        pltpu.VMEM((len(A_BRANCHES), A_SUPER, HEAD_DIM), F32)],
    )
    return pl.pallas_call(
        _dilated_kernel,
        out_shape=jax.ShapeDtypeStruct((B, S, A_W), F32),
        grid_spec=grid_spec,
        compiler_params=_cparams(("parallel", "parallel", "arbitrary")),
        name="dilated_attention",
    )(slopes, pa, pa, pa)


def _compress_kernel(kc_ref, vc_ref, wk_ref, wv_ref, pe_ref, ko_ref, vo_ref, xf, bsc, *, nc):
    for x_ref, w_ref, o_ref in ((kc_ref, wk_ref, ko_ref), (vc_ref, wv_ref, vo_ref)):
        xf[...] = x_ref[0].astype(F32)
        first = jnp.zeros((nc, HEAD_DIM), F32)
        second = jnp.zeros((nc, HEAD_DIM), F32)
        for j in range(CMP_STRIDE):
            xj = xf[pl.ds(j, nc, stride=CMP_STRIDE), :]
            wa = w_ref[j * HEAD_DIM:(j + 1) * HEAD_DIM, :].astype(BF16)
            wb = w_ref[(CMP_STRIDE + j) * HEAD_DIM:(CMP_STRIDE + j + 1) * HEAD_DIM, :].astype(BF16)
            first += jnp.dot((xj + pe_ref[j:j + 1, :]).astype(BF16), wa, preferred_element_type=F32)
            second += jnp.dot((xj + pe_ref[CMP_STRIDE + j:CMP_STRIDE + j + 1, :]).astype(BF16), wb,
                              preferred_element_type=F32)
        bsc[0:nc, :] = second
        bsc[nc:nc + 8, :] = jnp.zeros((8, HEAD_DIM), F32)
        out = first + bsc[1:nc + 1, :]
        row = lax.broadcasted_iota(jnp.int32, out.shape, 0)
        o_ref[0, 0] = jnp.where(row < nc - 1, out, 0.0).astype(o_ref.dtype)


def nsa_compress(pa, cmp_wk, cmp_wv, cmp_pe, layer):
    B, S, _ = pa.shape
    nc = S // CMP_STRIDE
    kc0 = (3 * A_W + N_W) // HEAD_DIM
    vc0 = kc0 + NSA_KV_HEADS
    wspec = pl.BlockSpec((None, CMP_LEN * HEAD_DIM, HEAD_DIM), lambda b, g: (layer, 0, 0))
    ospec = pl.BlockSpec((1, 1, nc, HEAD_DIM), lambda b, g: (b, g, 0, 0))
    oshape = jax.ShapeDtypeStruct((B, NSA_KV_HEADS, nc, HEAD_DIM), BF16)
    return pl.pallas_call(
        functools.partial(_compress_kernel, nc=nc),
        out_shape=[oshape, oshape],
        grid=(B, NSA_KV_HEADS),
        in_specs=[pl.BlockSpec((1, S, HEAD_DIM), lambda b, g: (b, 0, kc0 + g)),
                  pl.BlockSpec((1, S, HEAD_DIM), lambda b, g: (b, 0, vc0 + g)),
                  wspec, wspec,
                  pl.BlockSpec((None, CMP_LEN, HEAD_DIM), lambda b, g: (layer, 0, 0))],
        out_specs=[ospec, ospec],
        scratch_shapes=[pltpu.VMEM((S, HEAD_DIM), F32), pltpu.VMEM((nc + 8, HEAD_DIM), F32)],
        compiler_params=_cparams(("parallel", "parallel")),
        name="nsa_compress",
    )(pa, pa, cmp_wk, cmp_wv, cmp_pe)


def _nsa_kernel(sl_ref, q_ref, gt_ref, kc_ref, vc_ref, ks_ref, vs_ref, kw_ref, vw_ref, o_ref,
                m_sc, l_sc, acc_sc, *, TQ, nc):
    g = pl.program_id(1)
    qt = pl.program_id(2)
    q0 = qt * TQ
    M = NSA_GROUP
    R = M * TQ
    nt = (((1,), (1,)), ((), ()))
    kt = LANES
    blk_per_kt = kt // SEL_BLK

    q = q_ref[0]
    qs = jnp.concatenate([q[:, m * HEAD_DIM:(m + 1) * HEAD_DIM] for m in range(M)], axis=0)
    slope_col = jnp.concatenate([jnp.full((TQ, 1), sl_ref[g * M + m], F32) for m in range(M)], axis=0)
    tq_i = q0 + lax.broadcasted_iota(jnp.int32, (TQ, 1), 0)
    t_col = jnp.concatenate([tq_i] * M, axis=0).astype(F32)

    def softmax_rows(s, ok):
        s = jnp.where(ok, s, NEG)
        m = jnp.max(s, axis=-1, keepdims=True)
        e = jnp.where(ok, jnp.exp(s - m), 0.0)
        den = jnp.sum(e, axis=-1, keepdims=True)
        return e / jnp.maximum(den, TINY)

    cidx = lax.broadcasted_iota(jnp.int32, (1, nc), 1)
    c_end = (cidx * CMP_STRIDE + (CMP_LEN - 1)).astype(F32)
    dist = t_col - c_end
    s = lax.dot_general(qs, kc_ref[0, 0], nt, preferred_element_type=F32) - slope_col * dist
    p_cmp = softmax_rows(s, (dist >= 0) & (cidx < nc - 1))
    o_cmp = jnp.dot(p_cmp.astype(BF16), vc_ref[0, 0], preferred_element_type=F32)

    psum = p_cmp[0:TQ]
    for m in range(1, M):
        psum = psum + p_cmp[m * TQ:(m + 1) * TQ]
    ci = lax.broadcasted_iota(jnp.int32, (nc, LANES), 0)
    ni = lax.broadcasted_iota(jnp.int32, (nc, LANES), 1)
    overlap = ((ci * CMP_STRIDE < ni * SEL_BLK + SEL_BLK) & (ci * CMP_STRIDE + CMP_LEN - 1 >= ni * SEL_BLK)
               & (ci < nc - 1))
    overlap = jnp.where(overlap, 1.0, 0.0).astype(BF16)
    imp = jnp.zeros((TQ, LANES), F32)
    for piece in _split3(psum):
        imp = imp + jnp.dot(piece, overlap, preferred_element_type=F32)
    jb = lax.broadcasted_iota(jnp.int32, (TQ, LANES), 1)
    cur = tq_i // SEL_BLK
    forced = (jb == 0) | (jb == cur) | (jb == cur - 1)
    imp = jnp.where(forced, imp + FORCE, imp)
    causal_blk = jb * SEL_BLK <= tq_i
    imp = jnp.where(causal_blk, imp, NEG)

    jbf = jb.astype(F32)
    work = imp
    memb = jnp.zeros((TQ, LANES), F32)
    for _ in range(TOP_N):
        mx = jnp.max(work, axis=-1, keepdims=True)
        first = jnp.min(jnp.where(work == mx, jbf, float(LANES)), axis=-1, keepdims=True)
        pick = jbf == first
        memb = jnp.where(pick, 1.0, memb)
        work = jnp.where(pick, LOWEST, work)
    memb = jnp.where(causal_blk, memb, 0.0).astype(BF16)

    def sweep(k_ref, v_ref, j_lo, j_hi, selected):
        m_sc[...] = jnp.full((R, 1), NEG, F32)
        l_sc[...] = jnp.zeros((R, 1), F32)
        acc_sc[...] = jnp.zeros((R, HEAD_DIM), F32)

        def body(j, carry):
            rows = pl.ds(pl.multiple_of(j * kt, kt), kt)
            pos = (j * kt + lax.broadcasted_iota(jnp.int32, (1, kt), 1)).astype(F32)
            rel = t_col - pos
            s = lax.dot_general(qs, k_ref[0, rows, :], nt, preferred_element_type=F32) - slope_col * rel
            if selected:
                bi = lax.broadcasted_iota(jnp.int32, (LANES, kt), 0)
                ki = lax.broadcasted_iota(jnp.int32, (LANES, kt), 1)
                expand = jnp.where(bi == j * blk_per_kt + ki // SEL_BLK, 1.0, 0.0).astype(BF16)
                chosen = jnp.dot(memb, expand, preferred_element_type=F32)
                ok = (jnp.concatenate([chosen] * M, axis=0) > 0.5) & (rel >= 0)
            else:
                ok = (rel >= 0) & (rel < NSA_WIN)
            s = jnp.where(ok, s, NEG)
            m_old = m_sc[...]
            m_new = jnp.maximum(m_old, jnp.max(s, axis=-1, keepdims=True))
            a = jnp.exp(m_old - m_new)
            e = jnp.where(ok, jnp.exp(s - m_new), 0.0)
            l_sc[...] = a * l_sc[...] + jnp.sum(e, axis=-1, keepdims=True)
            acc_sc[...] = a * acc_sc[...] + jnp.dot(e.astype(BF16), v_ref[0, rows, :], preferred_element_type=F32)
            m_sc[...] = m_new
            return carry

        lax.fori_loop(j_lo, j_hi, body, 0)
        return acc_sc[...] / jnp.maximum(l_sc[...], TINY)

    j_end = (q0 + TQ) // kt
    o_sel = sweep(ks_ref, vs_ref, 0, j_end, True)
    o_win = sweep(kw_ref, vw_ref, jnp.maximum(q0 - NSA_WIN, 0) // kt, j_end, False)

    gt = gt_ref[0]
    sig = 1.0 / (1.0 + jnp.exp(-gt))
    lane = lax.broadcasted_iota(jnp.int32, sig.shape, 1)
    for m in range(M):
        col = (g * M + m) * 3
        w = [jnp.sum(jnp.where(lane == col + b, sig, 0.0), axis=-1, keepdims=True) for b in range(3)]
        rs = slice(m * TQ, (m + 1) * TQ)
        o_ref[0, :, m * HEAD_DIM:(m + 1) * HEAD_DIM] = w[0] * o_cmp[rs] + w[1] * o_sel[rs] + w[2] * o_win[rs]


def nsa_attention(pa, pb, k_cmp, v_cmp, slopes, *, TQ=128):
    B, S, _ = pa.shape
    nc = k_cmp.shape[2]
    M = NSA_GROUP
    assert S // SEL_BLK <= LANES and TQ % LANES == 0 and NSA_WIN % LANES == 0
    q0 = 3 * A_W // (M * HEAD_DIM)
    kv0 = (3 * A_W + N_W) // HEAD_DIM + 2 * NSA_KV_HEADS
    kv = lambda n: pl.BlockSpec((1, S, HEAD_DIM), lambda b, g, t, sl: (b, 0, kv0 + n * NSA_KV_HEADS + g))
    cmp_spec = pl.BlockSpec((1, 1, nc, HEAD_DIM), lambda b, g, t, sl: (b, g, 0, 0))
    grid_spec = pltpu.PrefetchScalarGridSpec(
        num_scalar_prefetch=1,
        grid=(B, NSA_KV_HEADS, S // TQ),
        in_specs=[pl.BlockSpec((1, TQ, M * HEAD_DIM), lambda b, g, t, sl: (b, t, q0 + g)),
                  pl.BlockSpec((1, TQ, LANES), lambda b, g, t, sl: (b, t, 0)),
                  cmp_spec, cmp_spec, kv(0), kv(1), kv(2), kv(3)],
        out_specs=pl.BlockSpec((1, TQ, M * HEAD_DIM), lambda b, g, t, sl: (b, t, g)),
        scratch_shapes=[pltpu.VMEM((M * TQ, 1), F32), pltpu.VMEM((M * TQ, 1), F32),
                        pltpu.VMEM((M * TQ, HEAD_DIM), F32)],
    )
    return pl.pallas_call(
        functools.partial(_nsa_kernel, TQ=TQ, nc=nc),
        out_shape=jax.ShapeDtypeStruct((B, S, N_W), F32),
        grid_spec=grid_spec,
        compiler_params=_cparams(("parallel", "parallel", "arbitrary")),
        name="nsa_attention",
    )(slopes, pa, pb, k_cmp, v_cmp, pa, pa, pa, pa)


def kernel(x, c, ada_w, ada_b, w_in, conv_w, cmp_wk, cmp_wv, cmp_pe, mix_g, w_out, ln1_g, ln1_b, ln2_g, ln2_b,
           ffn_w_gate, ffn_w_up, ffn_w_down, moe_router, moe_router_b, moe_w_gate, moe_w_up, moe_w_down):
    B, S, D = x.shape
    depth = ada_w.shape[0]
    C = conv_w.shape[-1]
    N = B * S
    alpha = (2 * depth) ** 0.25
    assert D == A_W + N_W + C and w_in.shape[-1] == ATT_W + GATE_W + 3 * C

    n_heads = A_HEADS + NSA_HEADS
    sl = jnp.exp2(-8.0 * jnp.arange(1, n_heads + 1, dtype=F32) / n_heads)
    sl_a, sl_n = sl[0::2], sl[1::2]
    scale = HEAD_DIM ** -0.5
    col = jnp.arange(ATT_W)
    is_q = (col < A_W) | ((col >= 3 * A_W) & (col < 3 * A_W + N_W))
    q_scale = jnp.where(is_q, scale, 1.0).astype(F32)[None, :]

    cond = jnp.zeros((8, D), F32).at[:B].set(c * jax.nn.sigmoid(c)).astype(BF16)
    mods = [matmul(cond, ada_w, i, 6 * D, tm=8, tn=1024)[:B] + ada_b[i] for i in range(depth)]
    h = None
    gate = None
    for i in range(depth):
        sh1, sc1, g1, sh2, sc2, g2 = [m[:, None, :] for m in jnp.split(mods[i], 6, axis=-1)]
        if i == 0:
            h = ln_mod(x, sc1, sh1)

        hf = h.reshape(N, D)
        pa = matmul(hf, w_in, i, ATT_W, scale=q_scale, out_dtype=BF16, tn=256).reshape(B, S, ATT_W)
        w_b = jnp.concatenate([w_in[i, :, ATT_W:ATT_W + GATE_W], jnp.zeros((D, LANES - GATE_W), F32),
                               w_in[i, :, ATT_W + GATE_W:]], axis=1)
        pb = matmul(hf, w_b, 0, LANES + 3 * C, tn=640).reshape(B, S, LANES + 3 * C)
        oa = dilated_attention(pa, sl_a)
        k_cmp, v_cmp = nsa_compress(pa, cmp_wk, cmp_wv, cmp_pe, i)
        on = nsa_attention(pa, pb, k_cmp, v_cmp, sl_n)
        oc = short_conv(pb, conv_w[i])
        y = out_proj(oa.reshape(N, A_W), on.reshape(N, N_W), oc.reshape(N, C), mix_g, w_out, i).reshape(B, S, D)

        moe_layer = i % 2 == 1
        j = i // 2
        if moe_layer:
            rw = jnp.zeros((D, LANES), F32).at[:, :N_EXPERTS].set(moe_router[j])
            rb = jnp.zeros((1, LANES), F32).at[0, :N_EXPERTS].set(moe_router_b[j])
            x, h, gate = res_ln(x, y, g1, ln1_g[i:i + 1], ln1_b[i:i + 1], alpha, sc=sc2, sh=sh2,
                                router_w=rw, router_b=rb)
        else:
            x, h = res_ln(x, y, g1, ln1_g[i:i + 1], ln1_b[i:i + 1], alpha, sc=sc2, sh=sh2)

        hf = h.reshape(N, D)
        if moe_layer:
            gate2 = gate.reshape(N, LANES)
            y = jnp.zeros((N, D), F32)
            for e in range(N_EXPERTS):
                a = gate_up(hf, moe_w_gate, moe_w_up, (j, e))
                y = down(a, moe_w_down, (j, e), gate=gate2, y_prev=y, expert=e)
        else:
            a = gate_up(hf, ffn_w_gate, ffn_w_up, (j,))
            F = a.shape[1]
            y = down(a, ffn_w_down, (j,), tn=256, nk=2 if F % 256 == 0 and F > 4096 else 1)
        y = y.reshape(B, S, D)

        if i + 1 < depth:
            sh_n, sc_n = mods[i + 1][:, None, :D], mods[i + 1][:, None, D:2 * D]
            x, h = res_ln(x, y, g2, ln2_g[i:i + 1], ln2_b[i:i + 1], alpha, sc=sc_n, sh=sh_n)
        else:
            (x,) = res_ln(x, y, g2, ln2_g[i:i + 1], ln2_b[i:i + 1], alpha)
    return x
```

```python
import functools

import jax
import jax.numpy as jnp
from jax import lax
from jax.experimental import pallas as pl
from jax.experimental.pallas import tpu as pltpu

F32 = jnp.float32
BF16 = jnp.bfloat16

HEAD_DIM = 128
A_HEADS = 12
A_BRANCHES = ((128, 1), (512, 4), (2048, 16))
A_BLK = 128
A_SUPER = A_BLK * max(d for _, d in A_BRANCHES)
NSA_HEADS = 12
NSA_KV_HEADS = 3
NSA_GROUP = NSA_HEADS // NSA_KV_HEADS
CMP_LEN = 32
CMP_STRIDE = 16
SEL_BLK = 64
TOP_N = 16
NSA_WIN = 512
CONV_W = 3
N_EXPERTS = 8
LN_EPS = 1e-5
NEG = -1e30
TINY = 1e-30
FORCE = 1e4
LOWEST = -3e38

A_W = A_HEADS * HEAD_DIM
N_W = NSA_HEADS * HEAD_DIM
KV_W = NSA_KV_HEADS * HEAD_DIM
ATT_W = 3 * A_W + N_W + 6 * KV_W
GATE_W = NSA_HEADS * 3
LANES = 128
VMEM_LIMIT = 56 * 1024 * 1024


def _cparams(sem):
    return pltpu.CompilerParams(dimension_semantics=sem, vmem_limit_bytes=VMEM_LIMIT)


def _pick(n, pref):
    if n <= pref:
        return n
    t = pref
    while n % t:
        t //= 2
    return t


def _mm_kernel(*refs, nk, has_scale):
    if has_scale:
        x_ref, w_ref, s_ref, o_ref, acc_ref = refs
    else:
        x_ref, w_ref, o_ref, acc_ref = refs
        s_ref = None
    part = jnp.dot(x_ref[...], w_ref[...].astype(BF16), preferred_element_type=F32)

    def finish(acc):
        if s_ref is not None:
            acc = acc * s_ref[...]
        o_ref[...] = acc.astype(o_ref.dtype)

    if nk == 1:
        finish(part)
        return
    k = pl.program_id(2)

    @pl.when(k == 0)
    def _():
        acc_ref[...] = part

    @pl.when(k > 0)
    def _():
        acc_ref[...] += part

    @pl.when(k == nk - 1)
    def _():
        finish(acc_ref[...])


def matmul(x, w, w_idx, n_cols, *, col0=0, scale=None, out_dtype=F32, tm=1024, tn=512, tk=None):
    M, K = x.shape
    tm = _pick(M, tm)
    tn = _pick(n_cols, tn)
    tk = K if tk is None else tk
    assert col0 % tn == 0 and K % tk == 0
    nk = K // tk
    cb = col0 // tn
    lead = (w_idx,) if w.ndim == 3 else ()
    wblk = ((None,) if w.ndim == 3 else ()) + (tk, tn)
    in_specs = [pl.BlockSpec((tm, tk), lambda i, j, k: (i, k)),
                pl.BlockSpec(wblk, lambda i, j, k: lead + (k, cb + j))]
    args = [x, w]
    if scale is not None:
        in_specs.append(pl.BlockSpec((1, tn), lambda i, j, k: (0, j)))
        args.append(scale)
    return pl.pallas_call(
        functools.partial(_mm_kernel, nk=nk, has_scale=scale is not None),
        out_shape=jax.ShapeDtypeStruct((M, n_cols), out_dtype),
        grid=(M // tm, n_cols // tn, nk),
        in_specs=in_specs,
        out_specs=pl.BlockSpec((tm, tn), lambda i, j, k: (i, j)),
        scratch_shapes=[pltpu.VMEM((tm, tn), F32)],
        compiler_params=_cparams(("parallel", "parallel", "arbitrary")),
        name="matmul",
    )(*args)


def _gate_up_kernel(x_ref, wg_ref, wu_ref, o_ref):
    x = x_ref[...]
    g = jnp.dot(x, wg_ref[...].astype(BF16), preferred_element_type=F32)
    u = jnp.dot(x, wu_ref[...].astype(BF16), preferred_element_type=F32)
    o_ref[...] = (g * (1.0 / (1.0 + jnp.exp(-g))) * u).astype(o_ref.dtype)


def gate_up(x, wg, wu, lead, *, tm=1024, tn=256):
    M, K = x.shape
    F = wg.shape[-1]
    tm = _pick(M, tm)
    tn = _pick(F, tn)
    nlead = len(lead)
    wspec = pl.BlockSpec((None,) * nlead + (K, tn), lambda i, j: lead + (0, j))
    return pl.pallas_call(
        _gate_up_kernel,
        out_shape=jax.ShapeDtypeStruct((M, F), BF16),
        grid=(M // tm, F // tn),
        in_specs=[pl.BlockSpec((tm, K), lambda i, j: (i, 0)), wspec, wspec],
        out_specs=pl.BlockSpec((tm, tn), lambda i, j: (i, j)),
        compiler_params=_cparams(("parallel", "parallel")),
        name="gate_up",
    )(x, wg, wu)


def moe_plan(table, tile_rows):
    N = table.shape[0]
    e = jnp.concatenate([table[:, 0], table[:, 1]]).astype(jnp.int32)
    w = jnp.concatenate([table[:, 2], table[:, 3]])
    onehot = (e[:, None] == jnp.arange(N_EXPERTS)[None, :]).astype(jnp.int32)
    csum = jnp.cumsum(onehot, axis=0)
    rank = jnp.sum(csum * onehot, axis=1) - 1
    tiles = (csum[-1] + tile_rows - 1) // tile_rows
    tile_end = jnp.cumsum(tiles)
    row = (tile_end - tiles)[e] * tile_rows + rank
    T = 2 * N // tile_rows + N_EXPERTS
    src = jnp.zeros((T * tile_rows,), jnp.int32).at[row].set(jnp.arange(2 * N, dtype=jnp.int32) % N)
    row_gate = jnp.zeros((T * tile_rows,), F32).at[row].set(w)
    n_used = tile_end[-1:].astype(jnp.int32)
    t_idx = jnp.minimum(jnp.arange(T), n_used[0] - 1)
    tile_expert = jnp.minimum(jnp.searchsorted(tile_end, t_idx, side="right"), N_EXPERTS - 1).astype(jnp.int32)
    return dict(src=src.reshape(T, 1, tile_rows), row_gate=row_gate.reshape(T * tile_rows, 1),
                tile_expert=tile_expert, n_used=n_used, rows=jnp.stack([row[:N], row[N:]], axis=0), T=T)


def _row_copy(src_hbm, row, dst, slot, sem):
    return pltpu.make_async_copy(src_hbm.at[pl.ds(row, 1), :], dst.at[pl.ds(slot, 1), :], sem)


def _moe_gate_up_kernel(te_ref, nu_ref, src_ref, h_hbm, wg_ref, wu_ref, o_ref, xf, xb, sem, *, TM, chunk):
    t = pl.program_id(0)
    used = t < nu_ref[0]

    @pl.when(used & (pl.program_id(1) == 0))
    def _():
        for part in range(TM // chunk):
            def issue(r, carry, part=part):
                _row_copy(h_hbm, src_ref[0, 0, part * chunk + r], xf, r, sem).start()
                return carry

            lax.fori_loop(0, chunk, issue, 0)
            pltpu.make_async_copy(h_hbm.at[pl.ds(0, chunk), :], xf, sem).wait()
            xb[part * chunk:(part + 1) * chunk, :] = xf[...].astype(BF16)

    @pl.when(used)
    def _():
        x = xb[...]
        g = jnp.dot(x, wg_ref[...].astype(BF16), preferred_element_type=F32)
        u = jnp.dot(x, wu_ref[...].astype(BF16), preferred_element_type=F32)
        o_ref[...] = (g * (1.0 / (1.0 + jnp.exp(-g))) * u).astype(o_ref.dtype)

    @pl.when(jnp.logical_not(used))
    def _():
        o_ref[...] = jnp.zeros_like(o_ref)


def moe_gate_up(h, wg, wu, layer, plan, *, TM, tn=256, chunk=512):
    N, K = h.shape
    F = wg.shape[-1]
    T = plan["T"]
    tn = _pick(F, tn)
    nj = F // tn
    wspec = pl.BlockSpec((None, None, K, tn),
                         lambda t, j, te, nu: (layer, te[t], 0, jnp.where(t < nu[0], j, nj - 1)))
    grid_spec = pltpu.PrefetchScalarGridSpec(
        num_scalar_prefetch=2,
        grid=(T, nj),
        in_specs=[pl.BlockSpec((1, 1, TM), lambda t, j, te, nu: (t, 0, 0), memory_space=pltpu.SMEM),
                  pl.BlockSpec(memory_space=pl.ANY), wspec, wspec],
        out_specs=pl.BlockSpec((TM, tn), lambda t, j, te, nu: (t, j)),
        scratch_shapes=[pltpu.VMEM((chunk, K), F32), pltpu.VMEM((TM, K), BF16), pltpu.SemaphoreType.DMA],
    )
    return pl.pallas_call(
        functools.partial(_moe_gate_up_kernel, TM=TM, chunk=chunk),
        out_shape=jax.ShapeDtypeStruct((T * TM, F), BF16),
        grid_spec=grid_spec,
        compiler_params=_cparams(("arbitrary", "arbitrary")),
        name="moe_gate_up",
    )(plan["tile_expert"], plan["n_used"], plan["src"], h, wg, wu)


def _moe_down_kernel(te_ref, nu_ref, a_ref, w_ref, rg_ref, o_ref):
    used = pl.program_id(0) < nu_ref[0]

    @pl.when(used)
    def _():
        o_ref[...] = rg_ref[...] * jnp.dot(a_ref[...], w_ref[...].astype(BF16), preferred_element_type=F32)

    @pl.when(jnp.logical_not(used))
    def _():
        o_ref[...] = jnp.zeros_like(o_ref)


def moe_down(a, wd, layer, plan, *, TM, tn=512):
    P, F = a.shape
    D = wd.shape[-1]
    T = plan["T"]
    tn = _pick(D, tn)
    nj = D // tn
    grid_spec = pltpu.PrefetchScalarGridSpec(
        num_scalar_prefetch=2,
        grid=(T, nj),
        in_specs=[pl.BlockSpec((TM, F), lambda t, j, te, nu: (t, 0)),
                  pl.BlockSpec((None, None, F, tn),
                               lambda t, j, te, nu: (layer, te[t], 0, jnp.where(t < nu[0], j, nj - 1))),
                  pl.BlockSpec((TM, 1), lambda t, j, te, nu: (t, 0))],
        out_specs=pl.BlockSpec((TM, tn), lambda t, j, te, nu: (t, j)),
    )
    return pl.pallas_call(
        _moe_down_kernel,
        out_shape=jax.ShapeDtypeStruct((P, D), F32),
        grid_spec=grid_spec,
        compiler_params=_cparams(("arbitrary", "arbitrary")),
        name="moe_down",
    )(plan["tile_expert"], plan["n_used"], a, wd, plan["row_gate"])


def _moe_combine_kernel(rows_ref, ye_hbm, o_ref, buf, sem, *, tm):
    def issue(r, carry):
        for c in range(2):
            _row_copy(ye_hbm, rows_ref[0, c, r], buf, c * tm + r, sem).start()
        return carry

    lax.fori_loop(0, tm, issue, 0)
    pltpu.make_async_copy(ye_hbm.at[pl.ds(0, 2 * tm), :], buf, sem).wait()
    o_ref[...] = buf[0:tm, :] + buf[tm:2 * tm, :]


def moe_combine(ye, plan, N, *, tm=256):
    D = ye.shape[1]
    tm = _pick(N, tm)
    rows = plan["rows"].reshape(2, N // tm, tm).transpose(1, 0, 2)
    return pl.pallas_call(
        functools.partial(_moe_combine_kernel, tm=tm),
        out_shape=jax.ShapeDtypeStruct((N, D), F32),
        grid=(N // tm,),
        in_specs=[pl.BlockSpec((1, 2, tm), lambda i: (i, 0, 0), memory_space=pltpu.SMEM),
                  pl.BlockSpec(memory_space=pl.ANY)],
        out_specs=pl.BlockSpec((tm, D), lambda i: (i, 0)),
        scratch_shapes=[pltpu.VMEM((2 * tm, D), F32), pltpu.SemaphoreType.DMA],
        compiler_params=_cparams(("arbitrary",)),
        name="moe_combine",
    )(rows, ye)


def _ln(v):
    mu = jnp.mean(v, axis=-1, keepdims=True)
    c = v - mu
    var = jnp.mean(c * c, axis=-1, keepdims=True)
    return c * lax.rsqrt(var + LN_EPS)


def _ln_mod_kernel(x_ref, sc_ref, sh_ref, h_ref):
    h_ref[0] = (_ln(x_ref[0]) * (1.0 + sc_ref[0]) + sh_ref[0]).astype(h_ref.dtype)


def ln_mod(x, sc, sh, *, tm=256):
    B, S, D = x.shape
    tm = _pick(S, tm)
    row = pl.BlockSpec((1, tm, D), lambda b, i: (b, i, 0))
    vec = pl.BlockSpec((1, 1, D), lambda b, i: (b, 0, 0))
    return pl.pallas_call(
        _ln_mod_kernel,
        out_shape=jax.ShapeDtypeStruct((B, S, D), BF16),
        grid=(B, S // tm),
        in_specs=[row, vec, vec],
        out_specs=row,
        compiler_params=_cparams(("parallel", "parallel")),
        name="ln_mod",
    )(x, sc, sh)


def _split3(v):
    hi = v.astype(BF16)
    r1 = v - hi.astype(F32)
    mid = r1.astype(BF16)
    lo = (r1 - mid.astype(F32)).astype(BF16)
    return hi, mid, lo


def _res_ln_kernel(*refs, alpha, with_h, with_router):
    it = iter(refs)
    x_ref, y_ref, g_ref, lng_ref, lnb_ref = (next(it) for _ in range(5))
    sc_ref = sh_ref = rw_ref = rb_ref = None
    if with_h:
        sc_ref, sh_ref = next(it), next(it)
    if with_router:
        rw_ref, rb_ref = next(it), next(it)
    xo_ref = next(it)
    h_ref = next(it) if with_h else None
    gate_ref = next(it) if with_router else None

    v = alpha * x_ref[0] + (1.0 + g_ref[0]) * y_ref[0]
    xn = _ln(v) * lng_ref[...] + lnb_ref[...]
    xo_ref[0] = xn
    if not with_h:
        return
    h = _ln(xn) * (1.0 + sc_ref[0]) + sh_ref[0]
    h_ref[0] = h.astype(h_ref.dtype)
    if not with_router:
        return
    hs = _split3(h)
    ws = _split3(rw_ref[...])
    logits = rb_ref[...]
    for a in range(3):
        for b in range(3 - a):
            logits = logits + jnp.dot(hs[a], ws[b], preferred_element_type=F32)
    lane = lax.broadcasted_iota(jnp.int32, logits.shape, 1).astype(F32)
    logits = jnp.where(lane < N_EXPERTS, logits, LOWEST)
    v1 = jnp.max(logits, axis=-1, keepdims=True)
    i1 = jnp.min(jnp.where(logits == v1, lane, float(LANES)), axis=-1, keepdims=True)
    rest = jnp.where(lane == i1, LOWEST, logits)
    v2 = jnp.max(rest, axis=-1, keepdims=True)
    i2 = jnp.min(jnp.where(rest == v2, lane, float(LANES)), axis=-1, keepdims=True)
    e2 = jnp.exp(v2 - v1)
    den = 1.0 + e2
    gate_ref[0] = jnp.where(lane == 0.0, i1, jnp.where(lane == 1.0, i2,
                            jnp.where(lane == 2.0, 1.0 / den, jnp.where(lane == 3.0, e2 / den, 0.0))))


def res_ln(x, y, g, lng, lnb, alpha, *, sc=None, sh=None, router_w=None, router_b=None, tm=256):
    B, S, D = x.shape
    tm = _pick(S, tm)
    with_h = sc is not None
    with_router = router_w is not None
    row = pl.BlockSpec((1, tm, D), lambda b, i: (b, i, 0))
    vec = pl.BlockSpec((1, 1, D), lambda b, i: (b, 0, 0))
    par = pl.BlockSpec((1, D), lambda b, i: (0, 0))
    in_specs = [row, row, vec, par, par]
    args = [x, y, g, lng, lnb]
    out_shape = [jax.ShapeDtypeStruct((B, S, D), F32)]
    out_specs = [row]
    if with_h:
        in_specs += [vec, vec]
        args += [sc, sh]
        out_shape.append(jax.ShapeDtypeStruct((B, S, D), F32 if with_router else BF16))
        out_specs.append(row)
    if with_router:
        in_specs += [pl.BlockSpec((D, LANES), lambda b, i: (0, 0)), pl.BlockSpec((1, LANES), lambda b, i: (0, 0))]
        args += [router_w, router_b]
        out_shape.append(jax.ShapeDtypeStruct((B, S, LANES), F32))
        out_specs.append(pl.BlockSpec((1, tm, LANES), lambda b, i: (b, i, 0)))
    return pl.pallas_call(
        functools.partial(_res_ln_kernel, alpha=alpha, with_h=with_h, with_router=with_router),
        out_shape=out_shape,
        grid=(B, S // tm),
        in_specs=in_specs,
        out_specs=out_specs,
        compiler_params=_cparams(("parallel", "parallel")),
        name="res_ln",
    )(*args)


def _rms(v, g):
    return (v * lax.rsqrt(jnp.mean(v * v, axis=-1, keepdims=True) + LN_EPS) * g).astype(BF16)


def _out_proj_kernel(oa_ref, on_ref, oc_ref, ga_ref, gn_ref, gc_ref, wa_ref, wn_ref, wc_ref, o_ref,
                     ya_sc, yn_sc, yc_sc):
    @pl.when(pl.program_id(1) == 0)
    def _():
        ya_sc[...] = _rms(oa_ref[...], ga_ref[...])
        yn_sc[...] = _rms(on_ref[...], gn_ref[...])
        yc_sc[...] = _rms(oc_ref[...], gc_ref[...])

    acc = jnp.dot(ya_sc[...], wa_ref[...].astype(BF16), preferred_element_type=F32)
    acc += jnp.dot(yn_sc[...], wn_ref[...].astype(BF16), preferred_element_type=F32)
    acc += jnp.dot(yc_sc[...], wc_ref[...].astype(BF16), preferred_element_type=F32)
    o_ref[...] = acc


def out_proj(oa, on, oc, mix_g, w_out, layer, *, tm=512, tn=512):
    M = oa.shape[0]
    C = oc.shape[1]
    D = w_out.shape[-1]
    tm = _pick(M, tm)
    tn = _pick(D, tn)
    assert A_W == N_W and (A_W + N_W) % C == 0
    ga = mix_g[layer:layer + 1, :A_W]
    gn = mix_g[layer:layer + 1, A_W:A_W + N_W]
    gc = mix_g[layer:layer + 1, A_W + N_W:]
    return pl.pallas_call(
        _out_proj_kernel,
        out_shape=jax.ShapeDtypeStruct((M, D), F32),
        grid=(M // tm, D // tn),
        in_specs=[pl.BlockSpec((tm, A_W), lambda i, j: (i, 0)),
                  pl.BlockSpec((tm, N_W), lambda i, j: (i, 0)),
                  pl.BlockSpec((tm, C), lambda i, j: (i, 0)),
                  pl.BlockSpec((1, A_W), lambda i, j: (0, 0)),
                  pl.BlockSpec((1, N_W), lambda i, j: (0, 0)),
                  pl.BlockSpec((1, C), lambda i, j: (0, 0)),
                  pl.BlockSpec((None, A_W, tn), lambda i, j: (layer, 0, j)),
                  pl.BlockSpec((None, N_W, tn), lambda i, j: (layer, 1, j)),
                  pl.BlockSpec((None, C, tn), lambda i, j: (layer, (A_W + N_W) // C, j))],
        out_specs=pl.BlockSpec((tm, tn), lambda i, j: (i, j)),
        scratch_shapes=[pltpu.VMEM((tm, A_W), BF16), pltpu.VMEM((tm, N_W), BF16), pltpu.VMEM((tm, C), BF16)],
        compiler_params=_cparams(("parallel", "arbitrary")),
        name="out_proj",
    )(oa, on, oc, ga, gn, gc, w_out, w_out, w_out)


def _conv_kernel(cur_ref, prev_ref, w_ref, o_ref, *, C):
    i = pl.program_id(1)
    cur = cur_ref[0]
    u, bg, cg = (cur[:, LANES + k * C:LANES + (k + 1) * C] for k in range(3))
    z = cg * u
    prev = prev_ref[0]
    zp = prev[:, LANES + 2 * C:LANES + 3 * C] * prev[:, LANES:LANES + C]
    zp = jnp.where(i > 0, zp, 0.0)
    rows = lax.broadcasted_iota(jnp.int32, z.shape, 0)
    z1 = jnp.where(rows >= 1, pltpu.roll(z, 1, axis=0), zp[7:8, :])
    z2 = jnp.where(rows >= 2, pltpu.roll(z, 2, axis=0), jnp.where(rows == 1, zp[7:8, :], zp[6:7, :]))
    w = w_ref[...]
    o_ref[0] = bg * (w[0:1, :] * z2 + w[1:2, :] * z1 + w[2:3, :] * z)


def short_conv(pb, conv_w_l, *, tm=512):
    B, S, W = pb.shape
    C = (W - LANES) // 3
    tm = _pick(S, tm)
    return pl.pallas_call(
        functools.partial(_conv_kernel, C=C),
        out_shape=jax.ShapeDtypeStruct((B, S, C), F32),
        grid=(B, S // tm),
        in_specs=[pl.BlockSpec((1, tm, W), lambda b, i: (b, i, 0)),
                  pl.BlockSpec((1, 8, W), lambda b, i: (b, jnp.maximum(i * (tm // 8) - 1, 0), 0)),
                  pl.BlockSpec((CONV_W, C), lambda b, i: (0, 0))],
        out_specs=pl.BlockSpec((1, tm, C), lambda b, i: (b, i, 0)),
        compiler_params=_cparams(("parallel", "parallel")),
        name="short_conv",
    )(pb, pb, conv_w_l)


def _dilated_kernel(sl_ref, q_ref, k_ref, v_ref, o_ref, qf, kf, vf, u_sc, m_sc, l_sc):
    h = pl.program_id(1)
    sb = pl.program_id(2)
    base = sb * A_SUPER

    @pl.when(sb == 0)
    def _():
        kf[...] = k_ref[0].astype(F32)
        vf[...] = v_ref[0].astype(F32)

    qf[...] = q_ref[0].astype(F32)
    slope = sl_ref[h]
    qi = lax.broadcasted_iota(jnp.int32, (A_BLK, A_BLK), 0)
    kj = lax.broadcasted_iota(jnp.int32, (A_BLK, A_BLK), 1)
    diff = (qi - kj).astype(F32)
    cur_ok = kj <= qi
    prev_band = kj >= qi
    nt = (((1,), (1,)), ((), ()))

    for br, (_, d) in enumerate(A_BRANCHES):
        span = A_BLK * d
        sd = slope * float(d)
        bias_c = sd * diff
        bias_p = sd * (diff + float(A_BLK))

        def step(i, carry, d=d, span=span, br=br, bias_c=bias_c, bias_p=bias_p):
            loc = (i // d) * span + (i % d)
            start = base + loc
            has_prev = start >= span
            pstart = jnp.where(has_prev, start - span, start)
            rows = pl.ds(start, A_BLK, stride=d) if d > 1 else pl.ds(start, A_BLK)
            prow = pl.ds(pstart, A_BLK, stride=d) if d > 1 else pl.ds(pstart, A_BLK)
            lrow = pl.ds(loc, A_BLK, stride=d) if d > 1 else pl.ds(loc, A_BLK)
            q = qf[lrow, :].astype(BF16)
            kc = kf[rows, :].astype(BF16)
            kp = kf[prow, :].astype(BF16)
            s_c = lax.dot_general(q, kc, nt, preferred_element_type=F32) - bias_c
            s_p = lax.dot_general(q, kp, nt, preferred_element_type=F32) - bias_p
            ok_p = prev_band & has_prev
            s_c = jnp.where(cur_ok, s_c, NEG)
            s_p = jnp.where(ok_p, s_p, NEG)
            m = jnp.maximum(jnp.max(s_c, axis=-1, keepdims=True), jnp.max(s_p, axis=-1, keepdims=True))
            e_c = jnp.where(cur_ok, jnp.exp(s_c - m), 0.0)
            e_p = jnp.where(ok_p, jnp.exp(s_p - m), 0.0)
            l = jnp.sum(e_c, axis=-1, keepdims=True) + jnp.sum(e_p, axis=-1, keepdims=True)
            u = jnp.dot(e_c.astype(BF16), vf[rows, :].astype(BF16), preferred_element_type=F32)
            u += jnp.dot(e_p.astype(BF16), vf[prow, :].astype(BF16), preferred_element_type=F32)
            u_sc[br, lrow, :] = u
            m_sc[br, lrow, :] = jnp.broadcast_to(m, (A_BLK, HEAD_DIM))
            l_sc[br, lrow, :] = jnp.broadcast_to(l, (A_BLK, HEAD_DIM))
            return carry

        lax.fori_loop(0, A_SUPER // A_BLK, step, 0, unroll=4)

    nb = len(A_BRANCHES)
    mt = m_sc[0]
    for br in range(1, nb):
        mt = jnp.maximum(mt, m_sc[br])
    num = jnp.zeros((A_SUPER, HEAD_DIM), F32)
    den = jnp.zeros((A_SUPER, HEAD_DIM), F32)
    for br in range(nb):
        w = jnp.exp(m_sc[br] - mt)
        num += w * u_sc[br]
        den += w * l_sc[br]
    o_ref[0] = num / den


def dilated_attention(pa, slopes):
    B, S, _ = pa.shape
    assert S % A_SUPER == 0
    grid_spec = pltpu.PrefetchScalarGridSpec(
        num_scalar_prefetch=1,
        grid=(B, A_HEADS, S // A_SUPER),
        in_specs=[pl.BlockSpec((1, A_SUPER, HEAD_DIM), lambda b, h, s, sl: (b, s, h)),
                  pl.BlockSpec((1, S, HEAD_DIM), lambda b, h, s, sl: (b, 0, A_HEADS + h)),
                  pl.BlockSpec((1, S, HEAD_DIM), lambda b, h, s, sl: (b, 0, 2 * A_HEADS + h))],
        out_specs=pl.BlockSpec((1, A_SUPER, HEAD_DIM), lambda b, h, s, sl: (b, s, h)),
        scratch_shapes=[pltpu.VMEM((A_SUPER, HEAD_DIM), F32),
                        pltpu.VMEM((S, HEAD_DIM), F32),
                        pltpu.VMEM((S, HEAD_DIM), F32),
                        pltpu.VMEM((len(A_BRANCHES), A_SUPER, HEAD_DIM), F32),
                        pltpu.VMEM((len(A_BRANCHES), A_SUPER, HEAD_DIM), F32),
                        pltpu.VMEM((len(A_BRANCHES), A_SUPER, HEAD_DIM), F32)],
    )
    return pl.pallas_call(
        _dilated_kernel,
        out_shape=jax.ShapeDtypeStruct((B, S, A_W), F32),
        grid_spec=grid_spec,
        compiler_params=_cparams(("parallel", "parallel", "arbitrary")),
        name="dilated_attention",
    )(slopes, pa, pa, pa)


def _compress_kernel(kc_ref, vc_ref, wk_ref, wv_ref, pe_ref, ko_ref, vo_ref, xf, bsc, *, nc):
    for x_ref, w_ref, o_ref in ((kc_ref, wk_ref, ko_ref), (vc_ref, wv_ref, vo_ref)):
        xf[...] = x_ref[0].astype(F32)
        first = jnp.zeros((nc, HEAD_DIM), F32)
        second = jnp.zeros((nc, HEAD_DIM), F32)
        for j in range(CMP_STRIDE):
            xj = xf[pl.ds(j, nc, stride=CMP_STRIDE), :]
            wa = w_ref[j * HEAD_DIM:(j + 1) * HEAD_DIM, :].astype(BF16)
            wb = w_ref[(CMP_STRIDE + j) * HEAD_DIM:(CMP_STRIDE + j + 1) * HEAD_DIM, :].astype(BF16)
            first += jnp.dot((xj + pe_ref[j:j + 1, :]).astype(BF16), wa, preferred_element_type=F32)
            second += jnp.dot((xj + pe_ref[CMP_STRIDE + j:CMP_STRIDE + j + 1, :]).astype(BF16), wb,
                              preferred_element_type=F32)
        bsc[0:nc, :] = second
        bsc[nc:nc + 8, :] = jnp.zeros((8, HEAD_DIM), F32)
        out = first + bsc[1:nc + 1, :]
        row = lax.broadcasted_iota(jnp.int32, out.shape, 0)
        o_ref[0, 0] = jnp.where(row < nc - 1, out, 0.0).astype(o_ref.dtype)


def nsa_compress(pa, cmp_wk, cmp_wv, cmp_pe, layer):
    B, S, _ = pa.shape
    nc = S // CMP_STRIDE
    kc0 = (3 * A_W + N_W) // HEAD_DIM
    vc0 = kc0 + NSA_KV_HEADS
    wspec = pl.BlockSpec((None, CMP_LEN * HEAD_DIM, HEAD_DIM), lambda b, g: (layer, 0, 0))
    ospec = pl.BlockSpec((1, 1, nc, HEAD_DIM), lambda b, g: (b, g, 0, 0))
    oshape = jax.ShapeDtypeStruct((B, NSA_KV_HEADS, nc, HEAD_DIM), BF16)
    return pl.pallas_call(
        functools.partial(_compress_kernel, nc=nc),
        out_shape=[oshape, oshape],
        grid=(B, NSA_KV_HEADS),
        in_specs=[pl.BlockSpec((1, S, HEAD_DIM), lambda b, g: (b, 0, kc0 + g)),
                  pl.BlockSpec((1, S, HEAD_DIM), lambda b, g: (b, 0, vc0 + g)),
                  wspec, wspec,
                  pl.BlockSpec((None, CMP_LEN, HEAD_DIM), lambda b, g: (layer, 0, 0))],
        out_specs=[ospec, ospec],
        scratch_shapes=[pltpu.VMEM((S, HEAD_DIM), F32), pltpu.VMEM((nc + 8, HEAD_DIM), F32)],
        compiler_params=_cparams(("parallel", "parallel")),
        name="nsa_compress",
    )(pa, pa, cmp_wk, cmp_wv, cmp_pe)


def _gate_cols(gt_ref, g, M):
    sig = 1.0 / (1.0 + jnp.exp(-gt_ref[0]))
    lane = lax.broadcasted_iota(jnp.int32, sig.shape, 1)
    return [[jnp.sum(jnp.where(lane == (g * M + m) * 3 + b, sig, 0.0), axis=-1, keepdims=True)
             for b in range(3)] for m in range(M)]


def _nsa_select_kernel(sl_ref, q_ref, gt_ref, kc_ref, vc_ref, op_ref, mn_ref, *, TQ, nc):
    g = pl.program_id(1)
    q0 = pl.program_id(2) * TQ
    M = NSA_GROUP
    nt = (((1,), (1,)), ((), ()))

    q = q_ref[0]
    qs = jnp.concatenate([q[:, m * HEAD_DIM:(m + 1) * HEAD_DIM] for m in range(M)], axis=0)
    slope_col = jnp.concatenate([jnp.full((TQ, 1), sl_ref[g * M + m], F32) for m in range(M)], axis=0)
    tq_i = q0 + lax.broadcasted_iota(jnp.int32, (TQ, 1), 0)
    t_col = jnp.concatenate([tq_i] * M, axis=0).astype(F32)

    def softmax_rows(s, ok):
        s = jnp.where(ok, s, NEG)
        m = jnp.max(s, axis=-1, keepdims=True)
        e = jnp.where(ok, jnp.exp(s - m), 0.0)
        den = jnp.sum(e, axis=-1, keepdims=True)
        return e / jnp.maximum(den, TINY)

    cidx = lax.broadcasted_iota(jnp.int32, (1, nc), 1)
    c_end = (cidx * CMP_STRIDE + (CMP_LEN - 1)).astype(F32)
    dist = t_col - c_end
    s = lax.dot_general(qs, kc_ref[0, 0], nt, preferred_element_type=F32) - slope_col * dist
    p_cmp = softmax_rows(s, (dist >= 0) & (cidx < nc - 1))
    o_cmp = jnp.dot(p_cmp.astype(BF16), vc_ref[0, 0], preferred_element_type=F32)

    psum = p_cmp[0:TQ]
    for m in range(1, M):
        psum = psum + p_cmp[m * TQ:(m + 1) * TQ]
    ci = lax.broadcasted_iota(jnp.int32, (nc, LANES), 0)
    ni = lax.broadcasted_iota(jnp.int32, (nc, LANES), 1)
    overlap = ((ci * CMP_STRIDE < ni * SEL_BLK + SEL_BLK) & (ci * CMP_STRIDE + CMP_LEN - 1 >= ni * SEL_BLK)
               & (ci < nc - 1))
    overlap = jnp.where(overlap, 1.0, 0.0).astype(BF16)
    imp = jnp.zeros((TQ, LANES), F32)
    for piece in _split3(psum):
        imp = imp + jnp.dot(piece, overlap, preferred_element_type=F32)
    jb = lax.broadcasted_iota(jnp.int32, (TQ, LANES), 1)
    cur = tq_i // SEL_BLK
    forced = (jb == 0) | (jb == cur) | (jb == cur - 1)
    imp = jnp.where(forced, imp + FORCE, imp)
    causal_blk = jb * SEL_BLK <= tq_i
    imp = jnp.where(causal_blk, imp, NEG)

    work = imp
    memb = jnp.zeros((TQ, LANES), F32)
    jbf = jb.astype(F32)
    for _ in range(TOP_N):
        mx = jnp.max(work, axis=-1, keepdims=True)
        first = jnp.min(jnp.where(work == mx, jbf, float(LANES)), axis=-1, keepdims=True)
        pick = jbf == first
        memb = jnp.where(pick, 1.0, memb)
        work = jnp.where(pick, LOWEST, work)
    chosen = (memb > 0.5) & causal_blk
    mn_ref[0, 0] = jnp.where(chosen, 0.0, NEG).astype(mn_ref.dtype)
    gates = _gate_cols(gt_ref, g, M)
    for m in range(M):
        op_ref[0, :, m * HEAD_DIM:(m + 1) * HEAD_DIM] = gates[m][0] * o_cmp[m * TQ:(m + 1) * TQ]


def nsa_select(pa, pb, k_cmp, v_cmp, slopes, *, TQ=512):
    B, S, _ = pa.shape
    nc = k_cmp.shape[2]
    M = NSA_GROUP
    TQ = _pick(S, TQ)
    assert S // SEL_BLK <= LANES
    q0 = 3 * A_W // (M * HEAD_DIM)
    cmp_spec = pl.BlockSpec((1, 1, nc, HEAD_DIM), lambda b, g, t, sl: (b, g, 0, 0))
    grid_spec = pltpu.PrefetchScalarGridSpec(
        num_scalar_prefetch=1,
        grid=(B, NSA_KV_HEADS, S // TQ),
        in_specs=[pl.BlockSpec((1, TQ, M * HEAD_DIM), lambda b, g, t, sl: (b, t, q0 + g)),
                  pl.BlockSpec((1, TQ, LANES), lambda b, g, t, sl: (b, t, 0)),
                  cmp_spec, cmp_spec],
        out_specs=[pl.BlockSpec((1, TQ, M * HEAD_DIM), lambda b, g, t, sl: (b, t, g)),
                   pl.BlockSpec((1, 1, TQ, LANES), lambda b, g, t, sl: (b, g, t, 0))],
    )
    return pl.pallas_call(
        functools.partial(_nsa_select_kernel, TQ=TQ, nc=nc),
        out_shape=[jax.ShapeDtypeStruct((B, S, N_W), F32),
                   jax.ShapeDtypeStruct((B, NSA_KV_HEADS, S, LANES), BF16)],
        grid_spec=grid_spec,
        compiler_params=_cparams(("parallel", "parallel", "parallel")),
        name="nsa_select",
    )(slopes, pa, pb, k_cmp, v_cmp)


def _nsa_sweep_kernel(sl_ref, q_ref, gt_ref, op_ref, mn_ref, ks_ref, vs_ref, kw_ref, vw_ref, o_ref,
                      qa_sc, m_sc, l_sc, acc_sc, flag_ref, *, TQ, n_kt):
    g = pl.program_id(1)
    qt = pl.program_id(2)
    M = NSA_GROUP
    R = M * TQ
    kt = TQ
    bpk = kt // SEL_BLK
    nt = (((1,), (1,)), ((), ()))

    q = q_ref[0]
    mn = mn_ref[0, 0]
    for m in range(M):
        qa_sc[m * TQ:(m + 1) * TQ, 0:HEAD_DIM] = q[:, m * HEAD_DIM:(m + 1) * HEAD_DIM]
        qa_sc[m * TQ:(m + 1) * TQ, HEAD_DIM:2 * HEAD_DIM] = mn
    slope_rep = jnp.concatenate([jnp.full((TQ, kt), sl_ref[g * M + m], F32) for m in range(M)], axis=0)
    trel = jnp.concatenate([lax.broadcasted_iota(jnp.int32, (TQ, 1), 0)] * M, axis=0)
    lane = lax.broadcasted_iota(jnp.int32, (1, kt), 1)

    cnt = jnp.sum(jnp.where(mn.astype(F32) == 0.0, 1.0, 0.0), axis=0, keepdims=True)
    cnt = jnp.broadcast_to(cnt, (8, LANES))
    per_tile = cnt
    for b in range(1, bpk):
        per_tile = per_tile + pltpu.roll(cnt, LANES - b, axis=1)
    for j in range(n_kt):
        flag_ref[j] = (per_tile[0, j * bpk] > 0.0).astype(jnp.int32)

    def reset():
        m_sc[...] = jnp.full((R, kt), 0.5 * NEG, F32)
        l_sc[...] = jnp.zeros((R, kt), F32)
        acc_sc[...] = jnp.zeros((R, HEAD_DIM), F32)

    def tile(j, k_ref, v_ref, blocks, mask):
        rows = pl.ds(pl.multiple_of(j * kt, kt), kt)
        k = k_ref[0, rows, :]
        if blocks:
            ki = lax.broadcasted_iota(jnp.int32, (kt, LANES), 0)
            bi = lax.broadcasted_iota(jnp.int32, (kt, LANES), 1)
            onehot = jnp.where(bi == j * bpk + ki // SEL_BLK, 1.0, 0.0).astype(BF16)
            s = lax.dot_general(qa_sc[...], jnp.concatenate([k, onehot], axis=1), nt, preferred_element_type=F32)
        else:
            s = lax.dot_general(qa_sc[:, 0:HEAD_DIM], k, nt, preferred_element_type=F32)
        prel = (j - qt) * kt + lane
        s = s + slope_rep * prel.astype(F32)
        if mask == "causal":
            s = jnp.where(prel <= trel, s, NEG)
        elif mask == "window":
            d = trel - prel
            s = jnp.where((d >= 0) & (d < NSA_WIN), s, NEG)
        m_old = m_sc[...]
        m_new = jnp.maximum(m_old, jnp.max(s, axis=-1, keepdims=True))
        a = jnp.exp(m_old - m_new)
        p = jnp.exp(s - m_new)
        l_sc[...] = a * l_sc[...] + jnp.sum(p, axis=-1, keepdims=True)
        acc_sc[...] = a * acc_sc[...] + jnp.dot(p.astype(BF16), v_ref[0, rows, :], preferred_element_type=F32)
        m_sc[...] = m_new

    def result():
        return acc_sc[...] / jnp.maximum(l_sc[...], TINY)

    reset()

    def sel_body(j, carry):
        @pl.when(flag_ref[j] > 0)
        def _():
            tile(j, ks_ref, vs_ref, True, None)
        return carry

    lax.fori_loop(0, qt, sel_body, 0)
    tile(qt, ks_ref, vs_ref, True, "causal")
    o_sel = result()

    reset()

    def win_body(j, carry):
        tile(j, kw_ref, vw_ref, False, "window")
        return carry

    lax.fori_loop(jnp.maximum(qt - NSA_WIN // kt, 0), qt + 1, win_body, 0)
    o_win = result()

    gates = _gate_cols(gt_ref, g, M)
    for m in range(M):
        rs = slice(m * TQ, (m + 1) * TQ)
        cs = slice(m * HEAD_DIM, (m + 1) * HEAD_DIM)
        o_ref[0, :, cs] = op_ref[0, :, cs] + gates[m][1] * o_sel[rs] + gates[m][2] * o_win[rs]


def nsa_sweep(pa, pb, o_part, mneg, slopes, *, TQ=128):
    B, S, _ = pa.shape
    M = NSA_GROUP
    assert TQ == LANES and TQ % SEL_BLK == 0 and NSA_WIN % TQ == 0 and S % TQ == 0
    q0 = 3 * A_W // (M * HEAD_DIM)
    kv0 = (3 * A_W + N_W) // HEAD_DIM + 2 * NSA_KV_HEADS
    kv = lambda n: pl.BlockSpec((1, S, HEAD_DIM), lambda b, g, t, sl: (b, 0, kv0 + n * NSA_KV_HEADS + g))
    qspec = pl.BlockSpec((1, TQ, M * HEAD_DIM), lambda b, g, t, sl: (b, t, q0 + g))
    ospec = pl.BlockSpec((1, TQ, M * HEAD_DIM), lambda b, g, t, sl: (b, t, g))
    grid_spec = pltpu.PrefetchScalarGridSpec(
        num_scalar_prefetch=1,
        grid=(B, NSA_KV_HEADS, S // TQ),
        in_specs=[qspec,
                  pl.BlockSpec((1, TQ, LANES), lambda b, g, t, sl: (b, t, 0)),
                  ospec,
                  pl.BlockSpec((1, 1, TQ, LANES), lambda b, g, t, sl: (b, g, t, 0)),
                  kv(0), kv(1), kv(2), kv(3)],
        out_specs=ospec,
        scratch_shapes=[pltpu.VMEM((M * TQ, 2 * HEAD_DIM), BF16),
                        pltpu.VMEM((M * TQ, TQ), F32), pltpu.VMEM((M * TQ, TQ), F32),
                        pltpu.VMEM((M * TQ, HEAD_DIM), F32),
                        pltpu.SMEM((S // TQ,), jnp.int32)],
    )
    return pl.pallas_call(
        functools.partial(_nsa_sweep_kernel, TQ=TQ, n_kt=S // TQ),
        out_shape=jax.ShapeDtypeStruct((B, S, N_W), F32),
        grid_spec=grid_spec,
        compiler_params=_cparams(("parallel", "parallel", "arbitrary")),
        name="nsa_sweep",
    )(slopes, pa, pb, o_part, mneg, pa, pa, pa, pa)


def kernel(x, c, ada_w, ada_b, w_in, conv_w, cmp_wk, cmp_wv, cmp_pe, mix_g, w_out, ln1_g, ln1_b, ln2_g, ln2_b,
           ffn_w_gate, ffn_w_up, ffn_w_down, moe_router, moe_router_b, moe_w_gate, moe_w_up, moe_w_down):
    B, S, D = x.shape
    depth = ada_w.shape[0]
    C = conv_w.shape[-1]
    N = B * S
    alpha = (2 * depth) ** 0.25
    assert D == A_W + N_W + C and w_in.shape[-1] == ATT_W + GATE_W + 3 * C

    n_heads = A_HEADS + NSA_HEADS
    sl = jnp.exp2(-8.0 * jnp.arange(1, n_heads + 1, dtype=F32) / n_heads)
    sl_a, sl_n = sl[0::2], sl[1::2]
    scale = HEAD_DIM ** -0.5
    col = jnp.arange(ATT_W)
    is_q = (col < A_W) | ((col >= 3 * A_W) & (col < 3 * A_W + N_W))
    q_scale = jnp.where(is_q, scale, 1.0).astype(F32)[None, :]

    cond = jnp.zeros((8, D), F32).at[:B].set(c * jax.nn.sigmoid(c)).astype(BF16)
    mods = [matmul(cond, ada_w, i, 6 * D, tm=8, tn=1024)[:B] + ada_b[i] for i in range(depth)]
    h = None
    gate = None
    for i in range(depth):
        sh1, sc1, g1, sh2, sc2, g2 = [m[:, None, :] for m in jnp.split(mods[i], 6, axis=-1)]
        if i == 0:
            h = ln_mod(x, sc1, sh1)

        hf = h.reshape(N, D)
        pa = matmul(hf, w_in, i, ATT_W, scale=q_scale, out_dtype=BF16, tn=256).reshape(B, S, ATT_W)
        w_b = jnp.concatenate([w_in[i, :, ATT_W:ATT_W + GATE_W], jnp.zeros((D, LANES - GATE_W), F32),
                               w_in[i, :, ATT_W + GATE_W:]], axis=1)
        pb = matmul(hf, w_b, 0, LANES + 3 * C, tn=640).reshape(B, S, LANES + 3 * C)
        oa = dilated_attention(pa, sl_a)
        k_cmp, v_cmp = nsa_compress(pa, cmp_wk, cmp_wv, cmp_pe, i)
        o_part, mneg = nsa_select(pa, pb, k_cmp, v_cmp, sl_n)
        on = nsa_sweep(pa, pb, o_part, mneg, sl_n)
        oc = short_conv(pb, conv_w[i])
        y = out_proj(oa.reshape(N, A_W), on.reshape(N, N_W), oc.reshape(N, C), mix_g, w_out, i).reshape(B, S, D)

        moe_layer = i % 2 == 1
        j = i // 2
        if moe_layer:
            rw = jnp.zeros((D, LANES), F32).at[:, :N_EXPERTS].set(moe_router[j])
            rb = jnp.zeros((1, LANES), F32).at[0, :N_EXPERTS].set(moe_router_b[j])
            x, h, table = res_ln(x, y, g1, ln1_g[i:i + 1], ln1_b[i:i + 1], alpha, sc=sc2, sh=sh2,
                                 router_w=rw, router_b=rb)
        else:
            x, h = res_ln(x, y, g1, ln1_g[i:i + 1], ln1_b[i:i + 1], alpha, sc=sc2, sh=sh2)

        hf = h.reshape(N, D)
        if moe_layer:
            TM = _pick(N, 1024)
            plan = moe_plan(table.reshape(N, LANES), TM)
            a = moe_gate_up(hf, moe_w_gate, moe_w_up, j, plan, TM=TM, chunk=_pick(TM, 512))
            ye = moe_down(a, moe_w_down, j, plan, TM=TM)
            y = moe_combine(ye, plan, N)
        else:
            a = gate_up(hf, ffn_w_gate, ffn_w_up, (j,))
            F = a.shape[1]
            y = matmul(a, ffn_w_down, j, D, tn=256, tk=F // 2 if F % 256 == 0 and F > 4096 else None)
        y = y.reshape(B, S, D)

        if i + 1 < depth:
            sh_n, sc_n = mods[i + 1][:, None, :D], mods[i + 1][:, None, D:2 * D]
            x, h = res_ln(x, y, g2, ln2_g[i:i + 1], ln2_b[i:i + 1], alpha, sc=sc_n, sh=sh_n)
        else:
            (x,) = res_ln(x, y, g2, ln2_g[i:i + 1], ln2_b[i:i + 1], alpha)
    return x
```

```python
import functools

import jax
import jax.numpy as jnp
from jax import lax
from jax.experimental import pallas as pl
from jax.experimental.pallas import tpu as pltpu

F32 = jnp.float32
BF16 = jnp.bfloat16

HEAD_DIM = 128
A_HEADS = 12
A_BRANCHES = ((128, 1), (512, 4), (2048, 16))
A_BLK = 128
A_SUPER = A_BLK * max(d for _, d in A_BRANCHES)
A_GROUP = 8
NSA_HEADS = 12
NSA_KV_HEADS = 3
NSA_GROUP = NSA_HEADS // NSA_KV_HEADS
CMP_LEN = 32
CMP_STRIDE = 16
SEL_BLK = 64
TOP_N = 16
NSA_WIN = 512
CONV_W = 3
N_EXPERTS = 8
LN_EPS = 1e-5
NEG = -1e30
TINY = 1e-30
FORCE = 1e4
LOWEST = -3e38

A_W = A_HEADS * HEAD_DIM
N_W = NSA_HEADS * HEAD_DIM
KV_W = NSA_KV_HEADS * HEAD_DIM
ATT_W = 3 * A_W + N_W + 6 * KV_W
GATE_W = NSA_HEADS * 3
LANES = 128
PB_TN = 3 * LANES
VMEM_LIMIT = 56 * 1024 * 1024


def _cparams(sem):
    return pltpu.CompilerParams(dimension_semantics=sem, vmem_limit_bytes=VMEM_LIMIT)


def _pick(n, pref):
    if n <= pref:
        return n
    t = pref
    while n % t:
        t //= 2
    return t


def _mm_kernel(*refs, nk, k_tail, n_tail, has_scale):
    if has_scale:
        x_ref, w_ref, s_ref, o_ref, acc_ref = refs
    else:
        x_ref, w_ref, o_ref, acc_ref = refs
        s_ref = None
    x = x_ref[...]
    w = w_ref[...]
    if n_tail:
        limit = jnp.where(pl.program_id(1) == pl.num_programs(1) - 1, n_tail, w.shape[1])
        w = jnp.where(lax.broadcasted_iota(jnp.int32, w.shape, 1) < limit, w, 0.0)
    if k_tail:
        tk = x.shape[1]
        limit = jnp.where(pl.program_id(2) == nk - 1, k_tail, tk)
        x = jnp.where(lax.broadcasted_iota(jnp.int32, x.shape, 1) < limit, x, jnp.zeros_like(x))
        w = jnp.where(lax.broadcasted_iota(jnp.int32, w.shape, 0) < limit, w, 0.0)
    part = jnp.dot(x, w.astype(BF16), preferred_element_type=F32)

    def finish(acc):
        if s_ref is not None:
            acc = acc * s_ref[...]
        o_ref[...] = acc.astype(o_ref.dtype)

    if nk == 1:
        finish(part)
        return
    k = pl.program_id(2)

    @pl.when(k == 0)
    def _():
        acc_ref[...] = part

    @pl.when(k > 0)
    def _():
        acc_ref[...] += part

    @pl.when(k == nk - 1)
    def _():
        finish(acc_ref[...])


def matmul(x, w, w_idx, n_cols, *, col0=0, scale=None, out_dtype=F32, tm=1024, tn=512, tk=None):
    M, K = x.shape
    tm = _pick(M, tm)
    tn = _pick(n_cols, tn)
    tk = K if tk is None or tk >= K else tk
    assert col0 % tn == 0
    nk = pl.cdiv(K, tk)
    cb = col0 // tn
    lead = (w_idx,) if w.ndim == 3 else ()
    wblk = ((None,) if w.ndim == 3 else ()) + (tk, tn)
    in_specs = [pl.BlockSpec((tm, tk), lambda i, j, k: (i, k)),
                pl.BlockSpec(wblk, lambda i, j, k: lead + (k, cb + j))]
    args = [x, w]
    if scale is not None:
        in_specs.append(pl.BlockSpec((1, tn), lambda i, j, k: (0, j)))
        args.append(scale)
    return pl.pallas_call(
        functools.partial(_mm_kernel, nk=nk, k_tail=K % tk, n_tail=(w.shape[-1] - col0) % tn if
                          col0 + n_cols > w.shape[-1] else 0, has_scale=scale is not None),
        out_shape=jax.ShapeDtypeStruct((M, n_cols), out_dtype),
        grid=(M // tm, n_cols // tn, nk),
        in_specs=in_specs,
        out_specs=pl.BlockSpec((tm, tn), lambda i, j, k: (i, j)),
        scratch_shapes=[pltpu.VMEM((tm, tn), F32)],
        compiler_params=_cparams(("parallel", "parallel", "arbitrary")),
        name="matmul",
    )(*args)


def _gate_up_kernel(x_ref, wg_ref, wu_ref, o_ref):
    x = x_ref[...]
    g = jnp.dot(x, wg_ref[...].astype(BF16), preferred_element_type=F32)
    u = jnp.dot(x, wu_ref[...].astype(BF16), preferred_element_type=F32)
    o_ref[...] = (g * (1.0 / (1.0 + jnp.exp(-g))) * u).astype(o_ref.dtype)


def gate_up(x, wg, wu, lead, *, tm=1024, tn=256):
    M, K = x.shape
    F = wg.shape[-1]
    tm = _pick(M, tm)
    tn = _pick(F, tn)
    nlead = len(lead)
    wspec = pl.BlockSpec((None,) * nlead + (K, tn), lambda i, j: lead + (0, j))
    return pl.pallas_call(
        _gate_up_kernel,
        out_shape=jax.ShapeDtypeStruct((M, F), BF16),
        grid=(M // tm, F // tn),
        in_specs=[pl.BlockSpec((tm, K), lambda i, j: (i, 0)), wspec, wspec],
        out_specs=pl.BlockSpec((tm, tn), lambda i, j: (i, j)),
        compiler_params=_cparams(("parallel", "parallel")),
        name="gate_up",
    )(x, wg, wu)


def moe_plan(table, tile_rows):
    N = table.shape[0]
    e = jnp.concatenate([table[:, 0], table[:, 1]]).astype(jnp.int32)
    w = jnp.concatenate([table[:, 2], table[:, 3]])
    onehot = (e[:, None] == jnp.arange(N_EXPERTS)[None, :]).astype(jnp.int32)
    csum = jnp.cumsum(onehot, axis=0)
    rank = jnp.sum(csum * onehot, axis=1) - 1
    tiles = (csum[-1] + tile_rows - 1) // tile_rows
    tile_end = jnp.cumsum(tiles)
    row = (tile_end - tiles)[e] * tile_rows + rank
    T = 2 * N // tile_rows + N_EXPERTS
    src = jnp.zeros((T * tile_rows,), jnp.int32).at[row].set(jnp.arange(2 * N, dtype=jnp.int32) % N)
    row_gate = jnp.zeros((T * tile_rows,), F32).at[row].set(w)
    n_used = tile_end[-1:].astype(jnp.int32)
    t_idx = jnp.minimum(jnp.arange(T), n_used[0] - 1)
    tile_expert = jnp.minimum(jnp.searchsorted(tile_end, t_idx, side="right"), N_EXPERTS - 1).astype(jnp.int32)
    return dict(src=src.reshape(T, 1, tile_rows), row_gate=row_gate.reshape(T * tile_rows, 1),
                tile_expert=tile_expert, n_used=n_used, rows=jnp.stack([row[:N], row[N:]], axis=0), T=T)


def _row_copy(src_hbm, row, dst, slot, sem):
    return pltpu.make_async_copy(src_hbm.at[pl.ds(row, 1), :], dst.at[pl.ds(slot, 1), :], sem)


def _moe_gate_up_kernel(te_ref, nu_ref, src_ref, h_hbm, wg_ref, wu_ref, o_ref, xf, xb, sem, *, TM, chunk):
    t = pl.program_id(0)
    used = t < nu_ref[0]

    @pl.when(used & (pl.program_id(1) == 0))
    def _():
        for part in range(TM // chunk):
            def issue(r, carry, part=part):
                _row_copy(h_hbm, src_ref[0, 0, part * chunk + r], xf, r, sem).start()
                return carry

            lax.fori_loop(0, chunk, issue, 0, unroll=8)
            pltpu.make_async_copy(h_hbm.at[pl.ds(0, chunk), :], xf, sem).wait()
            xb[part * chunk:(part + 1) * chunk, :] = xf[...].astype(BF16)

    @pl.when(used)
    def _():
        x = xb[...]
        g = jnp.dot(x, wg_ref[...].astype(BF16), preferred_element_type=F32)
        u = jnp.dot(x, wu_ref[...].astype(BF16), preferred_element_type=F32)
        o_ref[...] = (g * (1.0 / (1.0 + jnp.exp(-g))) * u).astype(o_ref.dtype)

    @pl.when(jnp.logical_not(used))
    def _():
        o_ref[...] = jnp.zeros_like(o_ref)


def moe_gate_up(h, wg, wu, layer, plan, *, TM, tn=256, chunk=512):
    N, K = h.shape
    F = wg.shape[-1]
    T = plan["T"]
    tn = _pick(F, tn)
    nj = F // tn
    wspec = pl.BlockSpec((None, None, K, tn),
                         lambda t, j, te, nu: (layer, te[t], 0, jnp.where(t < nu[0], j, nj - 1)))
    grid_spec = pltpu.PrefetchScalarGridSpec(
        num_scalar_prefetch=2,
        grid=(T, nj),
        in_specs=[pl.BlockSpec((1, 1, TM), lambda t, j, te, nu: (t, 0, 0), memory_space=pltpu.SMEM),
                  pl.BlockSpec(memory_space=pl.ANY), wspec, wspec],
        out_specs=pl.BlockSpec((TM, tn), lambda t, j, te, nu: (t, j)),
        scratch_shapes=[pltpu.VMEM((chunk, K), F32), pltpu.VMEM((TM, K), BF16), pltpu.SemaphoreType.DMA],
    )
    return pl.pallas_call(
        functools.partial(_moe_gate_up_kernel, TM=TM, chunk=chunk),
        out_shape=jax.ShapeDtypeStruct((T * TM, F), BF16),
        grid_spec=grid_spec,
        compiler_params=_cparams(("arbitrary", "arbitrary")),
        name="moe_gate_up",
    )(plan["tile_expert"], plan["n_used"], plan["src"], h, wg, wu)


def _moe_down_kernel(te_ref, nu_ref, a_ref, w_ref, rg_ref, o_ref):
    used = pl.program_id(0) < nu_ref[0]

    @pl.when(used)
    def _():
        o_ref[...] = rg_ref[...] * jnp.dot(a_ref[...], w_ref[...].astype(BF16), preferred_element_type=F32)

    @pl.when(jnp.logical_not(used))
    def _():
        o_ref[...] = jnp.zeros_like(o_ref)


def moe_down(a, wd, layer, plan, *, TM, tn=512):
    P, F = a.shape
    D = wd.shape[-1]
    T = plan["T"]
    tn = _pick(D, tn)
    nj = D // tn
    grid_spec = pltpu.PrefetchScalarGridSpec(
        num_scalar_prefetch=2,
        grid=(T, nj),
        in_specs=[pl.BlockSpec((TM, F), lambda t, j, te, nu: (t, 0)),
                  pl.BlockSpec((None, None, F, tn),
                               lambda t, j, te, nu: (layer, te[t], 0, jnp.where(t < nu[0], j, nj - 1))),
                  pl.BlockSpec((TM, 1), lambda t, j, te, nu: (t, 0))],
        out_specs=pl.BlockSpec((TM, tn), lambda t, j, te, nu: (t, j)),
    )
    return pl.pallas_call(
        _moe_down_kernel,
        out_shape=jax.ShapeDtypeStruct((P, D), F32),
        grid_spec=grid_spec,
        compiler_params=_cparams(("arbitrary", "arbitrary")),
        name="moe_down",
    )(plan["tile_expert"], plan["n_used"], a, wd, plan["row_gate"])


def _moe_combine_kernel(rows_ref, ye_hbm, o_ref, buf, sem, *, tm):
    def issue(r, carry):
        for c in range(2):
            _row_copy(ye_hbm, rows_ref[0, c, r], buf, c * tm + r, sem).start()
        return carry

    lax.fori_loop(0, tm, issue, 0, unroll=8)
    pltpu.make_async_copy(ye_hbm.at[pl.ds(0, 2 * tm), :], buf, sem).wait()
    o_ref[...] = buf[0:tm, :] + buf[tm:2 * tm, :]


def moe_combine(ye, plan, N, *, tm=256):
    D = ye.shape[1]
    tm = _pick(N, tm)
    rows = plan["rows"].reshape(2, N // tm, tm).transpose(1, 0, 2)
    return pl.pallas_call(
        functools.partial(_moe_combine_kernel, tm=tm),
        out_shape=jax.ShapeDtypeStruct((N, D), F32),
        grid=(N // tm,),
        in_specs=[pl.BlockSpec((1, 2, tm), lambda i: (i, 0, 0), memory_space=pltpu.SMEM),
                  pl.BlockSpec(memory_space=pl.ANY)],
        out_specs=pl.BlockSpec((tm, D), lambda i: (i, 0)),
        scratch_shapes=[pltpu.VMEM((2 * tm, D), F32), pltpu.SemaphoreType.DMA],
        compiler_params=_cparams(("arbitrary",)),
        name="moe_combine",
    )(rows, ye)


def _ln(v):
    mu = jnp.mean(v, axis=-1, keepdims=True)
    c = v - mu
    var = jnp.mean(c * c, axis=-1, keepdims=True)
    return c * lax.rsqrt(var + LN_EPS)


def _ln_mod_kernel(x_ref, sc_ref, sh_ref, h_ref):
    h_ref[0] = (_ln(x_ref[0]) * (1.0 + sc_ref[0]) + sh_ref[0]).astype(h_ref.dtype)


def ln_mod(x, sc, sh, *, tm=256):
    B, S, D = x.shape
    tm = _pick(S, tm)
    row = pl.BlockSpec((1, tm, D), lambda b, i: (b, i, 0))
    vec = pl.BlockSpec((1, 1, D), lambda b, i: (b, 0, 0))
    return pl.pallas_call(
        _ln_mod_kernel,
        out_shape=jax.ShapeDtypeStruct((B, S, D), BF16),
        grid=(B, S // tm),
        in_specs=[row, vec, vec],
        out_specs=row,
        compiler_params=_cparams(("parallel", "parallel")),
        name="ln_mod",
    )(x, sc, sh)


def _split3(v):
    hi = v.astype(BF16)
    r1 = v - hi.astype(F32)
    mid = r1.astype(BF16)
    lo = (r1 - mid.astype(F32)).astype(BF16)
    return hi, mid, lo


def _res_ln_kernel(*refs, alpha, with_h, with_router):
    it = iter(refs)
    x_ref, y_ref, g_ref, lng_ref, lnb_ref = (next(it) for _ in range(5))
    sc_ref = sh_ref = rw_ref = rb_ref = None
    if with_h:
        sc_ref, sh_ref = next(it), next(it)
    if with_router:
        rw_ref, rb_ref = next(it), next(it)
    xo_ref = next(it)
    h_ref = next(it) if with_h else None
    gate_ref = next(it) if with_router else None

    v = alpha * x_ref[0] + (1.0 + g_ref[0]) * y_ref[0]
    xn = _ln(v) * lng_ref[...] + lnb_ref[...]
    xo_ref[0] = xn
    if not with_h:
        return
    h = _ln(xn) * (1.0 + sc_ref[0]) + sh_ref[0]
    h_ref[0] = h.astype(h_ref.dtype)
    if not with_router:
        return
    hs = _split3(h)
    ws = _split3(rw_ref[...])
    logits = rb_ref[...]
    for a in range(3):
        for b in range(3 - a):
            logits = logits + jnp.dot(hs[a], ws[b], preferred_element_type=F32)
    lane = lax.broadcasted_iota(jnp.int32, logits.shape, 1).astype(F32)
    logits = jnp.where(lane < N_EXPERTS, logits, LOWEST)
    v1 = jnp.max(logits, axis=-1, keepdims=True)
    i1 = jnp.min(jnp.where(logits == v1, lane, float(LANES)), axis=-1, keepdims=True)
    rest = jnp.where(lane == i1, LOWEST, logits)
    v2 = jnp.max(rest, axis=-1, keepdims=True)
    i2 = jnp.min(jnp.where(rest == v2, lane, float(LANES)), axis=-1, keepdims=True)
    e2 = jnp.exp(v2 - v1)
    den = 1.0 + e2
    gate_ref[0] = jnp.where(lane == 0.0, i1, jnp.where(lane == 1.0, i2,
                            jnp.where(lane == 2.0, 1.0 / den, jnp.where(lane == 3.0, e2 / den, 0.0))))


def res_ln(x, y, g, lng, lnb, alpha, *, sc=None, sh=None, router_w=None, router_b=None, tm=256):
    B, S, D = x.shape
    tm = _pick(S, tm)
    with_h = sc is not None
    with_router = router_w is not None
    row = pl.BlockSpec((1, tm, D), lambda b, i: (b, i, 0))
    vec = pl.BlockSpec((1, 1, D), lambda b, i: (b, 0, 0))
    par = pl.BlockSpec((1, D), lambda b, i: (0, 0))
    in_specs = [row, row, vec, par, par]
    args = [x, y, g, lng, lnb]
    out_shape = [jax.ShapeDtypeStruct((B, S, D), F32)]
    out_specs = [row]
    if with_h:
        in_specs += [vec, vec]
        args += [sc, sh]
        out_shape.append(jax.ShapeDtypeStruct((B, S, D), F32 if with_router else BF16))
        out_specs.append(row)
    if with_router:
        in_specs += [pl.BlockSpec((D, LANES), lambda b, i: (0, 0)), pl.BlockSpec((1, LANES), lambda b, i: (0, 0))]
        args += [router_w, router_b]
        out_shape.append(jax.ShapeDtypeStruct((B, S, LANES), F32))
        out_specs.append(pl.BlockSpec((1, tm, LANES), lambda b, i: (b, i, 0)))
    return pl.pallas_call(
        functools.partial(_res_ln_kernel, alpha=alpha, with_h=with_h, with_router=with_router),
        out_shape=out_shape,
        grid=(B, S // tm),
        in_specs=in_specs,
        out_specs=out_specs,
        compiler_params=_cparams(("parallel", "parallel")),
        name="res_ln",
    )(*args)


def _rms(v, g):
    return (v * lax.rsqrt(jnp.mean(v * v, axis=-1, keepdims=True) + LN_EPS) * g).astype(BF16)


def _out_proj_kernel(oa_ref, on_ref, oc_ref, ga_ref, gn_ref, gc_ref, wa_ref, wn_ref, wc_ref, o_ref,
                     ya_sc, yn_sc, yc_sc):
    @pl.when(pl.program_id(1) == 0)
    def _():
        ya_sc[...] = _rms(oa_ref[...], ga_ref[...])
        yn_sc[...] = _rms(on_ref[...], gn_ref[...])
        yc_sc[...] = _rms(oc_ref[...], gc_ref[...])

    acc = jnp.dot(ya_sc[...], wa_ref[...].astype(BF16), preferred_element_type=F32)
    acc += jnp.dot(yn_sc[...], wn_ref[...].astype(BF16), preferred_element_type=F32)
    acc += jnp.dot(yc_sc[...], wc_ref[...].astype(BF16), preferred_element_type=F32)
    o_ref[...] = acc


def out_proj(oa, on, oc, mix_g, w_out, layer, *, tm=1024, tn=512):
    M = oa.shape[0]
    C = oc.shape[1]
    D = w_out.shape[-1]
    tm = _pick(M, tm)
    tn = _pick(D, tn)
    assert A_W == N_W and (A_W + N_W) % C == 0
    ga = mix_g[layer:layer + 1, :A_W]
    gn = mix_g[layer:layer + 1, A_W:A_W + N_W]
    gc = mix_g[layer:layer + 1, A_W + N_W:]
    return pl.pallas_call(
        _out_proj_kernel,
        out_shape=jax.ShapeDtypeStruct((M, D), F32),
        grid=(M // tm, D // tn),
        in_specs=[pl.BlockSpec((tm, A_W), lambda i, j: (i, 0), pipeline_mode=pl.Buffered(1)),
                  pl.BlockSpec((tm, N_W), lambda i, j: (i, 0), pipeline_mode=pl.Buffered(1)),
                  pl.BlockSpec((tm, C), lambda i, j: (i, 0), pipeline_mode=pl.Buffered(1)),
                  pl.BlockSpec((1, A_W), lambda i, j: (0, 0)),
                  pl.BlockSpec((1, N_W), lambda i, j: (0, 0)),
                  pl.BlockSpec((1, C), lambda i, j: (0, 0)),
                  pl.BlockSpec((None, A_W, tn), lambda i, j: (layer, 0, j)),
                  pl.BlockSpec((None, N_W, tn), lambda i, j: (layer, 1, j)),
                  pl.BlockSpec((None, C, tn), lambda i, j: (layer, (A_W + N_W) // C, j))],
        out_specs=pl.BlockSpec((tm, tn), lambda i, j: (i, j)),
        scratch_shapes=[pltpu.VMEM((tm, A_W), BF16), pltpu.VMEM((tm, N_W), BF16), pltpu.VMEM((tm, C), BF16)],
        compiler_params=_cparams(("parallel", "arbitrary")),
        name="out_proj",
    )(oa, on, oc, ga, gn, gc, w_out, w_out, w_out)


def _conv_kernel(cur_ref, prev_ref, w_ref, o_ref, *, C):
    i = pl.program_id(1)
    cur = cur_ref[0]
    u, bg, cg = (cur[:, GATE_W + k * C:GATE_W + (k + 1) * C] for k in range(3))
    z = cg * u
    prev = prev_ref[0]
    zp = prev[:, GATE_W + 2 * C:GATE_W + 3 * C] * prev[:, GATE_W:GATE_W + C]
    zp = jnp.where(i > 0, zp, 0.0)
    rows = lax.broadcasted_iota(jnp.int32, z.shape, 0)
    z1 = jnp.where(rows >= 1, pltpu.roll(z, 1, axis=0), zp[7:8, :])
    z2 = jnp.where(rows >= 2, pltpu.roll(z, 2, axis=0), jnp.where(rows == 1, zp[7:8, :], zp[6:7, :]))
    w = w_ref[...]
    o_ref[0] = bg * (w[0:1, :] * z2 + w[1:2, :] * z1 + w[2:3, :] * z)


def short_conv(pb, conv_w_l, *, tm=512):
    B, S, W = pb.shape
    C = conv_w_l.shape[-1]
    tm = _pick(S, tm)
    return pl.pallas_call(
        functools.partial(_conv_kernel, C=C),
        out_shape=jax.ShapeDtypeStruct((B, S, C), F32),
        grid=(B, S // tm),
        in_specs=[pl.BlockSpec((1, tm, W), lambda b, i: (b, i, 0)),
                  pl.BlockSpec((1, 8, W), lambda b, i: (b, jnp.maximum(i * (tm // 8) - 1, 0), 0)),
                  pl.BlockSpec((CONV_W, C), lambda b, i: (0, 0))],
        out_specs=pl.BlockSpec((1, tm, C), lambda b, i: (b, i, 0)),
        compiler_params=_cparams(("parallel", "parallel")),
        name="short_conv",
    )(pb, pb, conv_w_l)


def _dilated_kernel(sl_ref, q_ref, k_ref, v_ref, o_ref, qf, kf, vf, u_sc, m_sc, l_sc):
    h = pl.program_id(1)
    sb = pl.program_id(2)
    base = sb * A_SUPER

    @pl.when(sb == 0)
    def _():
        kf[...] = k_ref[0].astype(F32)
        vf[...] = v_ref[0].astype(F32)

    qf[...] = q_ref[0].astype(F32)
    slope = sl_ref[h]
    qi = lax.broadcasted_iota(jnp.int32, (A_BLK, A_BLK), 0)
    kj = lax.broadcasted_iota(jnp.int32, (A_BLK, A_BLK), 1)
    diff = (qi - kj).astype(F32)
    cur_ok = kj <= qi
    prev_band = kj >= qi
    nt = (((1,), (1,)), ((), ()))

    for br, (_, d) in enumerate(A_BRANCHES):
        span = A_BLK * d
        sd = slope * float(d)
        bias_c = sd * diff
        bias_p = sd * (diff + float(A_BLK))

        def ds(start):
            return pl.ds(start, A_BLK, stride=d) if d > 1 else pl.ds(start, A_BLK)

        for grp in range(A_SUPER // A_BLK // A_GROUP):
            blocks = []
            for i in range(grp * A_GROUP, (grp + 1) * A_GROUP):
                loc = (i // d) * span + (i % d)
                has_prev = (sb > 0) if loc < span else True
                start = base + loc
                pstart = jnp.where(has_prev, start - span, start) if loc < span else start - span
                blocks.append((ds(loc), ds(start), ds(pstart), has_prev))
            s_c, s_p, hp = [], [], []
            for lrow, rows, prow, has_prev in blocks:
                q = qf[lrow, :].astype(BF16)
                s_c.append(lax.dot_general(q, kf[rows, :].astype(BF16), nt, preferred_element_type=F32))
                s_p.append(lax.dot_general(q, kf[prow, :].astype(BF16), nt, preferred_element_type=F32))
                hp.append(jnp.full((1, 1), jnp.where(has_prev, 0.0, NEG), F32))
            s_c = jnp.where(cur_ok[None], jnp.stack(s_c) - bias_c[None], NEG)
            s_p = jnp.where(prev_band[None], jnp.stack(s_p) - bias_p[None] + jnp.stack(hp), NEG)
            m = jnp.max(jnp.maximum(s_c, s_p), axis=-1, keepdims=True)
            e_c = jnp.exp(s_c - m)
            e_p = jnp.exp(s_p - m)
            l = jnp.sum(e_c + e_p, axis=-1, keepdims=True)
            for n, (lrow, rows, prow, _) in enumerate(blocks):
                u = jnp.dot(e_c[n].astype(BF16), vf[rows, :].astype(BF16), preferred_element_type=F32)
                u += jnp.dot(e_p[n].astype(BF16), vf[prow, :].astype(BF16), preferred_element_type=F32)
                u_sc[br, lrow, :] = u
                m_sc[br, lrow, :] = jnp.broadcast_to(m[n], (A_BLK, HEAD_DIM))
                l_sc[br, lrow, :] = jnp.broadcast_to(l[n], (A_BLK, HEAD_DIM))

    nb = len(A_BRANCHES)
    mt = m_sc[0]
    for br in range(1, nb):
        mt = jnp.maximum(mt, m_sc[br])
    num = jnp.zeros((A_SUPER, HEAD_DIM), F32)
    den = jnp.zeros((A_SUPER, HEAD_DIM), F32)
    for br in range(nb):
        w = jnp.exp(m_sc[br] - mt)
        num += w * u_sc[br]
        den += w * l_sc[br]
    o_ref[0] = num / den


def dilated_attention(pa, slopes):
    B, S, _ = pa.shape
    assert S % A_SUPER == 0
    grid_spec = pltpu.PrefetchScalarGridSpec(
        num_scalar_prefetch=1,
        grid=(B, A_HEADS, S // A_SUPER),
        in_specs=[pl.BlockSpec((1, A_SUPER, HEAD_DIM), lambda b, h, s, sl: (b, s, h)),
                  pl.BlockSpec((1, S, HEAD_DIM), lambda b, h, s, sl: (b, 0, A_HEADS + h)),
                  pl.BlockSpec((1, S, HEAD_DIM), lambda b, h, s, sl: (b, 0, 2 * A_HEADS + h))],
        out_specs=pl.BlockSpec((1, A_SUPER, HEAD_DIM), lambda b, h, s, sl: (b, s, h)),
        scratch_shapes=[pltpu.VMEM((A_SUPER, HEAD_DIM), F32),
                        pltpu.VMEM((S, HEAD_DIM), F32),
                        pltpu.VMEM((S, HEAD_DIM), F32),
                        pltpu.VMEM((len(A_BRANCHES), A_SUPER, HEAD_DIM), F32),
                        pltpu.VMEM((len(A_BRANCHES), A_SUPER, HEAD_DIM), F32),
                        pltpu.VMEM((len(A_BRANCHES), A_SUPER, HEAD_DIM), F32)],
    )
    return pl.pallas_call(
        _dilated_kernel,
        out_shape=jax.ShapeDtypeStruct((B, S, A_W), F32),
        grid_spec=grid_spec,
        compiler_params=_cparams(("parallel", "parallel", "arbitrary")),
        name="dilated_attention",
    )(slopes, pa, pa, pa)


def _compress_kernel(kc_ref, vc_ref, wk_ref, wv_ref, pe_ref, ko_ref, vo_ref, xf, bsc, *, nc):
    for x_ref, w_ref, o_ref in ((kc_ref, wk_ref, ko_ref), (vc_ref, wv_ref, vo_ref)):
        xf[...] = x_ref[0].astype(F32)
        first = jnp.zeros((nc, HEAD_DIM), F32)
        second = jnp.zeros((nc, HEAD_DIM), F32)
        for j in range(CMP_STRIDE):
            xj = xf[pl.ds(j, nc, stride=CMP_STRIDE), :]
            wa = w_ref[j * HEAD_DIM:(j + 1) * HEAD_DIM, :].astype(BF16)
            wb = w_ref[(CMP_STRIDE + j) * HEAD_DIM:(CMP_STRIDE + j + 1) * HEAD_DIM, :].astype(BF16)
            first += jnp.dot((xj + pe_ref[j:j + 1, :]).astype(BF16), wa, preferred_element_type=F32)
            second += jnp.dot((xj + pe_ref[CMP_STRIDE + j:CMP_STRIDE + j + 1, :]).astype(BF16), wb,
                              preferred_element_type=F32)
        bsc[0:nc, :] = second
        bsc[nc:nc + 8, :] = jnp.zeros((8, HEAD_DIM), F32)
        out = first + bsc[1:nc + 1, :]
        row = lax.broadcasted_iota(jnp.int32, out.shape, 0)
        o_ref[0, 0] = jnp.where(row < nc - 1, out, 0.0).astype(o_ref.dtype)


def nsa_compress(pa, cmp_wk, cmp_wv, cmp_pe, layer):
    B, S, _ = pa.shape
    nc = S // CMP_STRIDE
    kc0 = (3 * A_W + N_W) // HEAD_DIM
    vc0 = kc0 + NSA_KV_HEADS
    wspec = pl.BlockSpec((None, CMP_LEN * HEAD_DIM, HEAD_DIM), lambda b, g: (layer, 0, 0))
    ospec = pl.BlockSpec((1, 1, nc, HEAD_DIM), lambda b, g: (b, g, 0, 0))
    oshape = jax.ShapeDtypeStruct((B, NSA_KV_HEADS, nc, HEAD_DIM), BF16)
    return pl.pallas_call(
        functools.partial(_compress_kernel, nc=nc),
        out_shape=[oshape, oshape],
        grid=(B, NSA_KV_HEADS),
        in_specs=[pl.BlockSpec((1, S, HEAD_DIM), lambda b, g: (b, 0, kc0 + g)),
                  pl.BlockSpec((1, S, HEAD_DIM), lambda b, g: (b, 0, vc0 + g)),
                  wspec, wspec,
                  pl.BlockSpec((None, CMP_LEN, HEAD_DIM), lambda b, g: (layer, 0, 0))],
        out_specs=[ospec, ospec],
        scratch_shapes=[pltpu.VMEM((S, HEAD_DIM), F32), pltpu.VMEM((nc + 8, HEAD_DIM), F32)],
        compiler_params=_cparams(("parallel", "parallel")),
        name="nsa_compress",
    )(pa, pa, cmp_wk, cmp_wv, cmp_pe)


def _gate_cols(gt_ref, g, M):
    sig = 1.0 / (1.0 + jnp.exp(-gt_ref[0]))
    lane = lax.broadcasted_iota(jnp.int32, sig.shape, 1)
    return [[jnp.sum(jnp.where(lane == (g * M + m) * 3 + b, sig, 0.0), axis=-1, keepdims=True)
             for b in range(3)] for m in range(M)]


def _nsa_select_kernel(sl_ref, q_ref, gt_ref, kc_ref, vc_ref, op_ref, mn_ref, *, TQ, nc):
    g = pl.program_id(1)
    q0 = pl.program_id(2) * TQ
    M = NSA_GROUP
    nt = (((1,), (1,)), ((), ()))

    q = q_ref[0]
    qs = jnp.concatenate([q[:, m * HEAD_DIM:(m + 1) * HEAD_DIM] for m in range(M)], axis=0)
    slope_col = jnp.concatenate([jnp.full((TQ, 1), sl_ref[g * M + m], F32) for m in range(M)], axis=0)
    tq_i = q0 + lax.broadcasted_iota(jnp.int32, (TQ, 1), 0)
    t_col = jnp.concatenate([tq_i] * M, axis=0).astype(F32)

    def softmax_rows(s, ok):
        s = jnp.where(ok, s, NEG)
        m = jnp.max(s, axis=-1, keepdims=True)
        e = jnp.where(ok, jnp.exp(s - m), 0.0)
        den = jnp.sum(e, axis=-1, keepdims=True)
        return e / jnp.maximum(den, TINY)

    cidx = lax.broadcasted_iota(jnp.int32, (1, nc), 1)
    c_end = (cidx * CMP_STRIDE + (CMP_LEN - 1)).astype(F32)
    dist = t_col - c_end
    s = lax.dot_general(qs, kc_ref[0, 0], nt, preferred_element_type=F32) - slope_col * dist
    p_cmp = softmax_rows(s, (dist >= 0) & (cidx < nc - 1))
    o_cmp = jnp.dot(p_cmp.astype(BF16), vc_ref[0, 0], preferred_element_type=F32)

    psum = p_cmp[0:TQ]
    for m in range(1, M):
        psum = psum + p_cmp[m * TQ:(m + 1) * TQ]
    ci = lax.broadcasted_iota(jnp.int32, (nc, LANES), 0)
    ni = lax.broadcasted_iota(jnp.int32, (nc, LANES), 1)
    overlap = ((ci * CMP_STRIDE < ni * SEL_BLK + SEL_BLK) & (ci * CMP_STRIDE + CMP_LEN - 1 >= ni * SEL_BLK)
               & (ci < nc - 1))
    overlap = jnp.where(overlap, 1.0, 0.0).astype(BF16)
    imp = jnp.zeros((TQ, LANES), F32)
    for piece in _split3(psum):
        imp = imp + jnp.dot(piece, overlap, preferred_element_type=F32)
    jb = lax.broadcasted_iota(jnp.int32, (TQ, LANES), 1)
    cur = tq_i // SEL_BLK
    forced = (jb == 0) | (jb == cur) | (jb == cur - 1)
    imp = jnp.where(forced, imp + FORCE, imp)
    causal_blk = jb * SEL_BLK <= tq_i
    imp = jnp.where(causal_blk, imp, NEG)

    work = imp
    memb = jnp.zeros((TQ, LANES), F32)
    jbf = jb.astype(F32)
    for _ in range(TOP_N):
        mx = jnp.max(work, axis=-1, keepdims=True)
        first = jnp.min(jnp.where(work == mx, jbf, float(LANES)), axis=-1, keepdims=True)
        pick = jbf == first
        memb = jnp.where(pick, 1.0, memb)
        work = jnp.where(pick, LOWEST, work)
    chosen = (memb > 0.5) & causal_blk
    mn_ref[0, 0] = jnp.where(chosen, 0.0, NEG).astype(mn_ref.dtype)
    gates = _gate_cols(gt_ref, g, M)
    for m in range(M):
        op_ref[0, :, m * HEAD_DIM:(m + 1) * HEAD_DIM] = gates[m][0] * o_cmp[m * TQ:(m + 1) * TQ]


def nsa_select(pa, pb, k_cmp, v_cmp, slopes, *, TQ=512):
    B, S, _ = pa.shape
    nc = k_cmp.shape[2]
    M = NSA_GROUP
    TQ = _pick(S, TQ)
    assert S // SEL_BLK <= LANES
    q0 = 3 * A_W // (M * HEAD_DIM)
    cmp_spec = pl.BlockSpec((1, 1, nc, HEAD_DIM), lambda b, g, t, sl: (b, g, 0, 0))
    grid_spec = pltpu.PrefetchScalarGridSpec(
        num_scalar_prefetch=1,
        grid=(B, NSA_KV_HEADS, S // TQ),
        in_specs=[pl.BlockSpec((1, TQ, M * HEAD_DIM), lambda b, g, t, sl: (b, t, q0 + g)),
                  pl.BlockSpec((1, TQ, LANES), lambda b, g, t, sl: (b, t, 0)),
                  cmp_spec, cmp_spec],
        out_specs=[pl.BlockSpec((1, TQ, M * HEAD_DIM), lambda b, g, t, sl: (b, t, g)),
                   pl.BlockSpec((1, 1, TQ, LANES), lambda b, g, t, sl: (b, g, t, 0))],
    )
    return pl.pallas_call(
        functools.partial(_nsa_select_kernel, TQ=TQ, nc=nc),
        out_shape=[jax.ShapeDtypeStruct((B, S, N_W), F32),
                   jax.ShapeDtypeStruct((B, NSA_KV_HEADS, S, LANES), BF16)],
        grid_spec=grid_spec,
        compiler_params=_cparams(("parallel", "parallel", "parallel")),
        name="nsa_select",
    )(slopes, pa, pb, k_cmp, v_cmp)


def _nsa_sweep_kernel(sl_ref, q_ref, gt_ref, op_ref, mn_ref, ks_ref, vs_ref, kw_ref, vw_ref, o_ref,
                      qa_sc, m_sc, l_sc, acc_sc, flag_ref, *, TQ, n_kt):
    g = pl.program_id(1)
    qt = pl.program_id(2)
    M = NSA_GROUP
    R = M * TQ
    kt = TQ
    bpk = kt // SEL_BLK
    nt = (((1,), (1,)), ((), ()))

    q = q_ref[0]
    mn = mn_ref[0, 0]
    for m in range(M):
        qa_sc[m * TQ:(m + 1) * TQ, 0:HEAD_DIM] = q[:, m * HEAD_DIM:(m + 1) * HEAD_DIM]
        qa_sc[m * TQ:(m + 1) * TQ, HEAD_DIM:2 * HEAD_DIM] = mn
    slope_rep = jnp.concatenate([jnp.full((TQ, kt), sl_ref[g * M + m], F32) for m in range(M)], axis=0)
    trel = jnp.concatenate([lax.broadcasted_iota(jnp.int32, (TQ, 1), 0)] * M, axis=0)
    lane = lax.broadcasted_iota(jnp.int32, (1, kt), 1)

    cnt = jnp.sum(jnp.where(mn.astype(F32) == 0.0, 1.0, 0.0), axis=0, keepdims=True)
    cnt = jnp.broadcast_to(cnt, (8, LANES))
    per_tile = cnt
    for b in range(1, bpk):
        per_tile = per_tile + pltpu.roll(cnt, LANES - b, axis=1)
    for j in range(n_kt):
        flag_ref[j] = (per_tile[0, j * bpk] > 0.0).astype(jnp.int32)

    def reset():
        m_sc[...] = jnp.full((R, kt), 0.5 * NEG, F32)
        l_sc[...] = jnp.zeros((R, kt), F32)
        acc_sc[...] = jnp.zeros((R, HEAD_DIM), F32)

    def tile(j, k_ref, v_ref, blocks, mask):
        rows = pl.ds(pl.multiple_of(j * kt, kt), kt)
        k = k_ref[0, rows, :]
        if blocks:
            ki = lax.broadcasted_iota(jnp.int32, (kt, LANES), 0)
            bi = lax.broadcasted_iota(jnp.int32, (kt, LANES), 1)
            onehot = jnp.where(bi == j * bpk + ki // SEL_BLK, 1.0, 0.0).astype(BF16)
            s = lax.dot_general(qa_sc[...], jnp.concatenate([k, onehot], axis=1), nt, preferred_element_type=F32)
        else:
            s = lax.dot_general(qa_sc[:, 0:HEAD_DIM], k, nt, preferred_element_type=F32)
        prel = (j - qt) * kt + lane
        s = s + slope_rep * prel.astype(F32)
        if mask == "causal":
            s = jnp.where(prel <= trel, s, NEG)
        elif mask == "window":
            d = trel - prel
            s = jnp.where((d >= 0) & (d < NSA_WIN), s, NEG)
        m_old = m_sc[...]
        m_new = jnp.maximum(m_old, jnp.max(s, axis=-1, keepdims=True))
        a = jnp.exp(m_old - m_new)
        p = jnp.exp(s - m_new)
        l_sc[...] = a * l_sc[...] + jnp.sum(p, axis=-1, keepdims=True)
        acc_sc[...] = a * acc_sc[...] + jnp.dot(p.astype(BF16), v_ref[0, rows, :], preferred_element_type=F32)
        m_sc[...] = m_new

    def result():
        return acc_sc[...] / jnp.maximum(l_sc[...], TINY)

    reset()

    def sel_body(j, carry):
        @pl.when(flag_ref[j] > 0)
        def _():
            tile(j, ks_ref, vs_ref, True, None)
        return carry

    lax.fori_loop(0, qt, sel_body, 0)
    tile(qt, ks_ref, vs_ref, True, "causal")
    o_sel = result()

    reset()

    def win_body(j, carry):
        tile(j, kw_ref, vw_ref, False, "window")
        return carry

    lax.fori_loop(jnp.maximum(qt - NSA_WIN // kt, 0), qt + 1, win_body, 0)
    o_win = result()

    gates = _gate_cols(gt_ref, g, M)
    for m in range(M):
        rs = slice(m * TQ, (m + 1) * TQ)
        cs = slice(m * HEAD_DIM, (m + 1) * HEAD_DIM)
        o_ref[0, :, cs] = op_ref[0, :, cs] + gates[m][1] * o_sel[rs] + gates[m][2] * o_win[rs]


def nsa_sweep(pa, pb, o_part, mneg, slopes, *, TQ=128):
    B, S, _ = pa.shape
    M = NSA_GROUP
    assert TQ == LANES and TQ % SEL_BLK == 0 and NSA_WIN % TQ == 0 and S % TQ == 0
    q0 = 3 * A_W // (M * HEAD_DIM)
    kv0 = (3 * A_W + N_W) // HEAD_DIM + 2 * NSA_KV_HEADS
    kv = lambda n: pl.BlockSpec((1, S, HEAD_DIM), lambda b, g, t, sl: (b, 0, kv0 + n * NSA_KV_HEADS + g))
    qspec = pl.BlockSpec((1, TQ, M * HEAD_DIM), lambda b, g, t, sl: (b, t, q0 + g))
    ospec = pl.BlockSpec((1, TQ, M * HEAD_DIM), lambda b, g, t, sl: (b, t, g))
    grid_spec = pltpu.PrefetchScalarGridSpec(
        num_scalar_prefetch=1,
        grid=(B, NSA_KV_HEADS, S // TQ),
        in_specs=[qspec,
                  pl.BlockSpec((1, TQ, LANES), lambda b, g, t, sl: (b, t, 0)),
                  ospec,
                  pl.BlockSpec((1, 1, TQ, LANES), lambda b, g, t, sl: (b, g, t, 0)),
                  kv(0), kv(1), kv(2), kv(3)],
        out_specs=ospec,
        scratch_shapes=[pltpu.VMEM((M * TQ, 2 * HEAD_DIM), BF16),
                        pltpu.VMEM((M * TQ, TQ), F32), pltpu.VMEM((M * TQ, TQ), F32),
                        pltpu.VMEM((M * TQ, HEAD_DIM), F32),
                        pltpu.SMEM((S // TQ,), jnp.int32)],
    )
    return pl.pallas_call(
        functools.partial(_nsa_sweep_kernel, TQ=TQ, n_kt=S // TQ),
        out_shape=jax.ShapeDtypeStruct((B, S, N_W), F32),
        grid_spec=grid_spec,
        compiler_params=_cparams(("parallel", "parallel", "arbitrary")),
        name="nsa_sweep",
    )(slopes, pa, pb, o_part, mneg, pa, pa, pa, pa)


def kernel(x, c, ada_w, ada_b, w_in, conv_w, cmp_wk, cmp_wv, cmp_pe, mix_g, w_out, ln1_g, ln1_b, ln2_g, ln2_b,
           ffn_w_gate, ffn_w_up, ffn_w_down, moe_router, moe_router_b, moe_w_gate, moe_w_up, moe_w_down):
    B, S, D = x.shape
    depth = ada_w.shape[0]
    C = conv_w.shape[-1]
    N = B * S
    alpha = (2 * depth) ** 0.25
    assert D == A_W + N_W + C and w_in.shape[-1] == ATT_W + GATE_W + 3 * C and ATT_W % PB_TN == 0
    pb_w = pl.cdiv(GATE_W + 3 * C, PB_TN) * PB_TN

    n_heads = A_HEADS + NSA_HEADS
    sl = jnp.exp2(-8.0 * jnp.arange(1, n_heads + 1, dtype=F32) / n_heads)
    sl_a, sl_n = sl[0::2], sl[1::2]
    scale = HEAD_DIM ** -0.5
    col = jnp.arange(ATT_W)
    is_q = (col < A_W) | ((col >= 3 * A_W) & (col < 3 * A_W + N_W))
    q_scale = jnp.where(is_q, scale, 1.0).astype(F32)[None, :]

    cond = jnp.zeros((8, D), F32).at[:B].set(c * jax.nn.sigmoid(c)).astype(BF16)
    mods = [matmul(cond, ada_w, i, 6 * D, tm=8, tn=1024)[:B] + ada_b[i] for i in range(depth)]
    h = None
    gate = None
    for i in range(depth):
        sh1, sc1, g1, sh2, sc2, g2 = [m[:, None, :] for m in jnp.split(mods[i], 6, axis=-1)]
        if i == 0:
            h = ln_mod(x, sc1, sh1)

        hf = h.reshape(N, D)
        pa = matmul(hf, w_in, i, ATT_W, scale=q_scale, out_dtype=BF16, tn=256).reshape(B, S, ATT_W)
        pb = matmul(hf, w_in, i, pb_w, col0=ATT_W, tn=PB_TN).reshape(B, S, pb_w)
        oa = dilated_attention(pa, sl_a)
        k_cmp, v_cmp = nsa_compress(pa, cmp_wk, cmp_wv, cmp_pe, i)
        o_part, mneg = nsa_select(pa, pb, k_cmp, v_cmp, sl_n)
        on = nsa_sweep(pa, pb, o_part, mneg, sl_n)
        oc = short_conv(pb, conv_w[i])
        y = out_proj(oa.reshape(N, A_W), on.reshape(N, N_W), oc.reshape(N, C), mix_g, w_out, i).reshape(B, S, D)

        moe_layer = i % 2 == 1
        j = i // 2
        if moe_layer:
            rw = jnp.zeros((D, LANES), F32).at[:, :N_EXPERTS].set(moe_router[j])
            rb = jnp.zeros((1, LANES), F32).at[0, :N_EXPERTS].set(moe_router_b[j])
            x, h, table = res_ln(x, y, g1, ln1_g[i:i + 1], ln1_b[i:i + 1], alpha, sc=sc2, sh=sh2,
                                 router_w=rw, router_b=rb)
        else:
            x, h = res_ln(x, y, g1, ln1_g[i:i + 1], ln1_b[i:i + 1], alpha, sc=sc2, sh=sh2)

        hf = h.reshape(N, D)
        if moe_layer:
            TM = _pick(N, 1024)
            plan = moe_plan(table.reshape(N, LANES), TM)
            a = moe_gate_up(hf, moe_w_gate, moe_w_up, j, plan, TM=TM, chunk=_pick(TM, 512))
            ye = moe_down(a, moe_w_down, j, plan, TM=TM)
            y = moe_combine(ye, plan, N)
        else:
            a = gate_up(hf, ffn_w_gate, ffn_w_up, (j,))
            y = matmul(a, ffn_w_down, j, D, tm=2048, tn=512, tk=1408)
        y = y.reshape(B, S, D)

        if i + 1 < depth:
            sh_n, sc_n = mods[i + 1][:, None, :D], mods[i + 1][:, None, D:2 * D]
            x, h = res_ln(x, y, g2, ln2_g[i:i + 1], ln2_b[i:i + 1], alpha, sc=sc_n, sh=sh_n)
        else:
            (x,) = res_ln(x, y, g2, ln2_g[i:i + 1], ln2_b[i:i + 1], alpha)
    return x
```

```python
import functools

import jax
import jax.numpy as jnp
from jax import lax
from jax.experimental import pallas as pl
from jax.experimental.pallas import tpu as pltpu

F32 = jnp.float32
BF16 = jnp.bfloat16

HEAD_DIM = 128
A_HEADS = 12
A_BRANCHES = ((128, 1), (512, 4), (2048, 16))
A_BLK = 128
A_SUPER = A_BLK * max(d for _, d in A_BRANCHES)
A_GROUP = 8
NSA_HEADS = 12
NSA_KV_HEADS = 3
NSA_GROUP = NSA_HEADS // NSA_KV_HEADS
CMP_LEN = 32
CMP_STRIDE = 16
SEL_BLK = 64
TOP_N = 16
NSA_WIN = 512
CONV_W = 3
N_EXPERTS = 8
LN_EPS = 1e-5
NEG = -1e30
TINY = 1e-30
FORCE = 1e4
LOWEST = -3e38

A_W = A_HEADS * HEAD_DIM
N_W = NSA_HEADS * HEAD_DIM
KV_W = NSA_KV_HEADS * HEAD_DIM
ATT_W = 3 * A_W + N_W + 6 * KV_W
GATE_W = NSA_HEADS * 3
LANES = 128
PB_TN = 3 * LANES
VMEM_LIMIT = 56 * 1024 * 1024


def _cparams(sem):
    return pltpu.CompilerParams(dimension_semantics=sem, vmem_limit_bytes=VMEM_LIMIT)


def _pick(n, pref):
    if n <= pref:
        return n
    t = pref
    while n % t:
        t //= 2
    return t


def _mm_kernel(*refs, nk, k_tail, n_tail, has_scale):
    if has_scale:
        x_ref, w_ref, s_ref, o_ref, acc_ref = refs
    else:
        x_ref, w_ref, o_ref, acc_ref = refs
        s_ref = None
    x = x_ref[...]
    w = w_ref[...]
    if n_tail:
        limit = jnp.where(pl.program_id(1) == pl.num_programs(1) - 1, n_tail, w.shape[1])
        w = jnp.where(lax.broadcasted_iota(jnp.int32, w.shape, 1) < limit, w, 0.0)
    if k_tail:
        tk = x.shape[1]
        limit = jnp.where(pl.program_id(2) == nk - 1, k_tail, tk)
        x = jnp.where(lax.broadcasted_iota(jnp.int32, x.shape, 1) < limit, x, jnp.zeros_like(x))
        w = jnp.where(lax.broadcasted_iota(jnp.int32, w.shape, 0) < limit, w, 0.0)
    part = jnp.dot(x, w.astype(BF16), preferred_element_type=F32)

    def finish(acc):
        if s_ref is not None:
            acc = acc * s_ref[...]
        o_ref[...] = acc.astype(o_ref.dtype)

    if nk == 1:
        finish(part)
        return
    k = pl.program_id(2)

    @pl.when(k == 0)
    def _():
        acc_ref[...] = part

    @pl.when(k > 0)
    def _():
        acc_ref[...] += part

    @pl.when(k == nk - 1)
    def _():
        finish(acc_ref[...])


def matmul(x, w, w_idx, n_cols, *, col0=0, scale=None, out_dtype=F32, tm=1024, tn=512, tk=None):
    M, K = x.shape
    tm = _pick(M, tm)
    tn = _pick(n_cols, tn)
    tk = K if tk is None or tk >= K else tk
    assert col0 % tn == 0
    nk = pl.cdiv(K, tk)
    cb = col0 // tn
    lead = (w_idx,) if w.ndim == 3 else ()
    wblk = ((None,) if w.ndim == 3 else ()) + (tk, tn)
    in_specs = [pl.BlockSpec((tm, tk), lambda i, j, k: (i, k)),
                pl.BlockSpec(wblk, lambda i, j, k: lead + (k, cb + j))]
    args = [x, w]
    if scale is not None:
        in_specs.append(pl.BlockSpec((1, tn), lambda i, j, k: (0, j)))
        args.append(scale)
    return pl.pallas_call(
        functools.partial(_mm_kernel, nk=nk, k_tail=K % tk, n_tail=(w.shape[-1] - col0) % tn if
                          col0 + n_cols > w.shape[-1] else 0, has_scale=scale is not None),
        out_shape=jax.ShapeDtypeStruct((M, n_cols), out_dtype),
        grid=(M // tm, n_cols // tn, nk),
        in_specs=in_specs,
        out_specs=pl.BlockSpec((tm, tn), lambda i, j, k: (i, j)),
        scratch_shapes=[pltpu.VMEM((tm, tn), F32)],
        compiler_params=_cparams(("parallel", "parallel", "arbitrary")),
        name="matmul",
    )(*args)


def _gate_up_kernel(x_ref, wg_ref, wu_ref, o_ref):
    x = x_ref[...]
    g = jnp.dot(x, wg_ref[...].astype(BF16), preferred_element_type=F32)
    u = jnp.dot(x, wu_ref[...].astype(BF16), preferred_element_type=F32)
    o_ref[...] = (g * (1.0 / (1.0 + jnp.exp(-g))) * u).astype(o_ref.dtype)


def gate_up(x, wg, wu, lead, *, tm=1024, tn=256):
    M, K = x.shape
    F = wg.shape[-1]
    tm = _pick(M, tm)
    tn = _pick(F, tn)
    nlead = len(lead)
    wspec = pl.BlockSpec((None,) * nlead + (K, tn), lambda i, j: lead + (0, j))
    return pl.pallas_call(
        _gate_up_kernel,
        out_shape=jax.ShapeDtypeStruct((M, F), BF16),
        grid=(M // tm, F // tn),
        in_specs=[pl.BlockSpec((tm, K), lambda i, j: (i, 0)), wspec, wspec],
        out_specs=pl.BlockSpec((tm, tn), lambda i, j: (i, j)),
        compiler_params=_cparams(("parallel", "parallel")),
        name="gate_up",
    )(x, wg, wu)


def moe_plan(table, tile_rows):
    N = table.shape[0]
    e = jnp.concatenate([table[:, 0], table[:, 1]]).astype(jnp.int32)
    w = jnp.concatenate([table[:, 2], table[:, 3]])
    onehot = (e[:, None] == jnp.arange(N_EXPERTS)[None, :]).astype(jnp.int32)
    csum = jnp.cumsum(onehot, axis=0)
    rank = jnp.sum(csum * onehot, axis=1) - 1
    tiles = (csum[-1] + tile_rows - 1) // tile_rows
    tile_end = jnp.cumsum(tiles)
    row = (tile_end - tiles)[e] * tile_rows + rank
    T = 2 * N // tile_rows + N_EXPERTS
    tok = (jnp.arange(2 * N, dtype=jnp.int32) % N).astype(F32)
    both = jnp.zeros((T * tile_rows, 2), F32).at[row].set(jnp.stack([tok, w], axis=1))
    src = both[:, 0].astype(jnp.int32)
    row_gate = both[:, 1]
    n_used = tile_end[-1:].astype(jnp.int32)
    t_idx = jnp.minimum(jnp.arange(T), n_used[0] - 1)
    tile_expert = jnp.minimum(jnp.searchsorted(tile_end, t_idx, side="right"), N_EXPERTS - 1).astype(jnp.int32)
    return dict(src=src.reshape(T, 1, tile_rows), row_gate=row_gate.reshape(T * tile_rows, 1),
                tile_expert=tile_expert, n_used=n_used, rows=jnp.stack([row[:N], row[N:]], axis=0), T=T)


def _row_copy(src_hbm, row, dst, slot, sem):
    return pltpu.make_async_copy(src_hbm.at[pl.ds(row, 1), :], dst.at[pl.ds(slot, 1), :], sem)


def _moe_gate_up_kernel(te_ref, nu_ref, src_ref, h_hbm, wg_ref, wu_ref, o_ref, xf, xb, sem, *, TM, chunk):
    t = pl.program_id(0)
    used = t < nu_ref[0]

    @pl.when(used & (pl.program_id(1) == 0))
    def _():
        for part in range(TM // chunk):
            def issue(r, carry, part=part):
                _row_copy(h_hbm, src_ref[0, 0, part * chunk + r], xf, r, sem).start()
                return carry

            lax.fori_loop(0, chunk, issue, 0, unroll=8)
            pltpu.make_async_copy(h_hbm.at[pl.ds(0, chunk), :], xf, sem).wait()
            xb[part * chunk:(part + 1) * chunk, :] = xf[...].astype(BF16)

    @pl.when(used)
    def _():
        x = xb[...]
        g = jnp.dot(x, wg_ref[...].astype(BF16), preferred_element_type=F32)
        u = jnp.dot(x, wu_ref[...].astype(BF16), preferred_element_type=F32)
        o_ref[...] = (g * (1.0 / (1.0 + jnp.exp(-g))) * u).astype(o_ref.dtype)

    @pl.when(jnp.logical_not(used))
    def _():
        o_ref[...] = jnp.zeros_like(o_ref)


def moe_gate_up(h, wg, wu, layer, plan, *, TM, tn=256, chunk=512):
    N, K = h.shape
    F = wg.shape[-1]
    T = plan["T"]
    tn = _pick(F, tn)
    nj = F // tn
    wspec = pl.BlockSpec((None, None, K, tn),
                         lambda t, j, te, nu: (layer, te[t], 0, jnp.where(t < nu[0], j, nj - 1)))
    grid_spec = pltpu.PrefetchScalarGridSpec(
        num_scalar_prefetch=2,
        grid=(T, nj),
        in_specs=[pl.BlockSpec((1, 1, TM), lambda t, j, te, nu: (t, 0, 0), memory_space=pltpu.SMEM),
                  pl.BlockSpec(memory_space=pl.ANY), wspec, wspec],
        out_specs=pl.BlockSpec((TM, tn), lambda t, j, te, nu: (t, j)),
        scratch_shapes=[pltpu.VMEM((chunk, K), F32), pltpu.VMEM((TM, K), BF16), pltpu.SemaphoreType.DMA],
    )
    return pl.pallas_call(
        functools.partial(_moe_gate_up_kernel, TM=TM, chunk=chunk),
        out_shape=jax.ShapeDtypeStruct((T * TM, F), BF16),
        grid_spec=grid_spec,
        compiler_params=_cparams(("arbitrary", "arbitrary")),
        name="moe_gate_up",
    )(plan["tile_expert"], plan["n_used"], plan["src"], h, wg, wu)


def _moe_down_kernel(te_ref, nu_ref, a_ref, w_ref, rg_ref, o_ref):
    used = pl.program_id(0) < nu_ref[0]

    @pl.when(used)
    def _():
        o_ref[...] = rg_ref[...] * jnp.dot(a_ref[...], w_ref[...].astype(BF16), preferred_element_type=F32)

    @pl.when(jnp.logical_not(used))
    def _():
        o_ref[...] = jnp.zeros_like(o_ref)


def moe_down(a, wd, layer, plan, *, TM, tn=512):
    P, F = a.shape
    D = wd.shape[-1]
    T = plan["T"]
    tn = _pick(D, tn)
    nj = D // tn
    grid_spec = pltpu.PrefetchScalarGridSpec(
        num_scalar_prefetch=2,
        grid=(T, nj),
        in_specs=[pl.BlockSpec((TM, F), lambda t, j, te, nu: (t, 0)),
                  pl.BlockSpec((None, None, F, tn),
                               lambda t, j, te, nu: (layer, te[t], 0, jnp.where(t < nu[0], j, nj - 1))),
                  pl.BlockSpec((TM, 1), lambda t, j, te, nu: (t, 0))],
        out_specs=pl.BlockSpec((TM, tn), lambda t, j, te, nu: (t, j)),
    )
    return pl.pallas_call(
        _moe_down_kernel,
        out_shape=jax.ShapeDtypeStruct((P, D), F32),
        grid_spec=grid_spec,
        compiler_params=_cparams(("arbitrary", "arbitrary")),
        name="moe_down",
    )(plan["tile_expert"], plan["n_used"], a, wd, plan["row_gate"])


def _moe_combine_kernel(rows_ref, ye_hbm, o_ref, buf, sem, *, tm):
    def issue(r, carry):
        for c in range(2):
            _row_copy(ye_hbm, rows_ref[0, c, r], buf, c * tm + r, sem).start()
        return carry

    lax.fori_loop(0, tm, issue, 0, unroll=8)
    pltpu.make_async_copy(ye_hbm.at[pl.ds(0, 2 * tm), :], buf, sem).wait()
    o_ref[...] = buf[0:tm, :] + buf[tm:2 * tm, :]


def moe_combine(ye, plan, N, *, tm=256):
    D = ye.shape[1]
    tm = _pick(N, tm)
    rows = plan["rows"].reshape(2, N // tm, tm).transpose(1, 0, 2)
    return pl.pallas_call(
        functools.partial(_moe_combine_kernel, tm=tm),
        out_shape=jax.ShapeDtypeStruct((N, D), F32),
        grid=(N // tm,),
        in_specs=[pl.BlockSpec((1, 2, tm), lambda i: (i, 0, 0), memory_space=pltpu.SMEM),
                  pl.BlockSpec(memory_space=pl.ANY)],
        out_specs=pl.BlockSpec((tm, D), lambda i: (i, 0)),
        scratch_shapes=[pltpu.VMEM((2 * tm, D), F32), pltpu.SemaphoreType.DMA],
        compiler_params=_cparams(("arbitrary",)),
        name="moe_combine",
    )(rows, ye)


def _ln(v):
    mu = jnp.mean(v, axis=-1, keepdims=True)
    c = v - mu
    var = jnp.mean(c * c, axis=-1, keepdims=True)
    return c * lax.rsqrt(var + LN_EPS)


def _ln_mod_kernel(x_ref, sc_ref, sh_ref, h_ref):
    h_ref[0] = (_ln(x_ref[0]) * (1.0 + sc_ref[0]) + sh_ref[0]).astype(h_ref.dtype)


def ln_mod(x, sc, sh, *, tm=256):
    B, S, D = x.shape
    tm = _pick(S, tm)
    row = pl.BlockSpec((1, tm, D), lambda b, i: (b, i, 0))
    vec = pl.BlockSpec((1, 1, D), lambda b, i: (b, 0, 0))
    return pl.pallas_call(
        _ln_mod_kernel,
        out_shape=jax.ShapeDtypeStruct((B, S, D), BF16),
        grid=(B, S // tm),
        in_specs=[row, vec, vec],
        out_specs=row,
        compiler_params=_cparams(("parallel", "parallel")),
        name="ln_mod",
    )(x, sc, sh)


def _split3(v):
    hi = v.astype(BF16)
    r1 = v - hi.astype(F32)
    mid = r1.astype(BF16)
    lo = (r1 - mid.astype(F32)).astype(BF16)
    return hi, mid, lo


def _res_ln_kernel(*refs, alpha, with_h, with_router):
    it = iter(refs)
    x_ref, y_ref, g_ref, lng_ref, lnb_ref = (next(it) for _ in range(5))
    sc_ref = sh_ref = rw_ref = rb_ref = None
    if with_h:
        sc_ref, sh_ref = next(it), next(it)
    if with_router:
        rw_ref, rb_ref = next(it), next(it)
    xo_ref = next(it)
    h_ref = next(it) if with_h else None
    gate_ref = next(it) if with_router else None

    v = alpha * x_ref[0] + (1.0 + g_ref[0]) * y_ref[0]
    xn = _ln(v) * lng_ref[...] + lnb_ref[...]
    xo_ref[0] = xn
    if not with_h:
        return
    h = _ln(xn) * (1.0 + sc_ref[0]) + sh_ref[0]
    h_ref[0] = h.astype(h_ref.dtype)
    if not with_router:
        return
    hs = _split3(h)
    ws = _split3(rw_ref[...])
    logits = rb_ref[...]
    for a in range(3):
        for b in range(3 - a):
            logits = logits + jnp.dot(hs[a], ws[b], preferred_element_type=F32)
    lane = lax.broadcasted_iota(jnp.int32, logits.shape, 1).astype(F32)
    logits = jnp.where(lane < N_EXPERTS, logits, LOWEST)
    v1 = jnp.max(logits, axis=-1, keepdims=True)
    i1 = jnp.min(jnp.where(logits == v1, lane, float(LANES)), axis=-1, keepdims=True)
    rest = jnp.where(lane == i1, LOWEST, logits)
    v2 = jnp.max(rest, axis=-1, keepdims=True)
    i2 = jnp.min(jnp.where(rest == v2, lane, float(LANES)), axis=-1, keepdims=True)
    e2 = jnp.exp(v2 - v1)
    den = 1.0 + e2
    gate_ref[0] = jnp.where(lane == 0.0, i1, jnp.where(lane == 1.0, i2,
                            jnp.where(lane == 2.0, 1.0 / den, jnp.where(lane == 3.0, e2 / den, 0.0))))


def res_ln(x, y, g, lng, lnb, alpha, *, sc=None, sh=None, router_w=None, router_b=None, tm=256):
    B, S, D = x.shape
    tm = _pick(S, tm)
    with_h = sc is not None
    with_router = router_w is not None
    row = pl.BlockSpec((1, tm, D), lambda b, i: (b, i, 0))
    vec = pl.BlockSpec((1, 1, D), lambda b, i: (b, 0, 0))
    par = pl.BlockSpec((1, D), lambda b, i: (0, 0))
    in_specs = [row, row, vec, par, par]
    args = [x, y, g, lng, lnb]
    out_shape = [jax.ShapeDtypeStruct((B, S, D), F32)]
    out_specs = [row]
    if with_h:
        in_specs += [vec, vec]
        args += [sc, sh]
        out_shape.append(jax.ShapeDtypeStruct((B, S, D), F32 if with_router else BF16))
        out_specs.append(row)
    if with_router:
        in_specs += [pl.BlockSpec((D, LANES), lambda b, i: (0, 0)), pl.BlockSpec((1, LANES), lambda b, i: (0, 0))]
        args += [router_w, router_b]
        out_shape.append(jax.ShapeDtypeStruct((B, S, LANES), F32))
        out_specs.append(pl.BlockSpec((1, tm, LANES), lambda b, i: (b, i, 0)))
    return pl.pallas_call(
        functools.partial(_res_ln_kernel, alpha=alpha, with_h=with_h, with_router=with_router),
        out_shape=out_shape,
        grid=(B, S // tm),
        in_specs=in_specs,
        out_specs=out_specs,
        compiler_params=_cparams(("parallel", "parallel")),
        name="res_ln",
    )(*args)


def _rms(v, g):
    return (v * lax.rsqrt(jnp.mean(v * v, axis=-1, keepdims=True) + LN_EPS) * g).astype(BF16)


def _out_proj_kernel(oa_ref, on_ref, oc_ref, ga_ref, gn_ref, gc_ref, wa_ref, wn_ref, wc_ref, o_ref,
                     ya_sc, yn_sc, yc_sc):
    @pl.when(pl.program_id(1) == 0)
    def _():
        ya_sc[...] = _rms(oa_ref[...], ga_ref[...])
        yn_sc[...] = _rms(on_ref[...], gn_ref[...])
        yc_sc[...] = _rms(oc_ref[...], gc_ref[...])

    acc = jnp.dot(ya_sc[...], wa_ref[...].astype(BF16), preferred_element_type=F32)
    acc += jnp.dot(yn_sc[...], wn_ref[...].astype(BF16), preferred_element_type=F32)
    acc += jnp.dot(yc_sc[...], wc_ref[...].astype(BF16), preferred_element_type=F32)
    o_ref[...] = acc


def out_proj(oa, on, oc, mix_g, w_out, layer, *, tm=1024, tn=512):
    M = oa.shape[0]
    C = oc.shape[1]
    D = w_out.shape[-1]
    tm = _pick(M, tm)
    tn = _pick(D, tn)
    assert A_W == N_W and (A_W + N_W) % C == 0
    ga = mix_g[layer:layer + 1, :A_W]
    gn = mix_g[layer:layer + 1, A_W:A_W + N_W]
    gc = mix_g[layer:layer + 1, A_W + N_W:]
    return pl.pallas_call(
        _out_proj_kernel,
        out_shape=jax.ShapeDtypeStruct((M, D), F32),
        grid=(M // tm, D // tn),
        in_specs=[pl.BlockSpec((tm, A_W), lambda i, j: (i, 0), pipeline_mode=pl.Buffered(1)),
                  pl.BlockSpec((tm, N_W), lambda i, j: (i, 0), pipeline_mode=pl.Buffered(1)),
                  pl.BlockSpec((tm, C), lambda i, j: (i, 0), pipeline_mode=pl.Buffered(1)),
                  pl.BlockSpec((1, A_W), lambda i, j: (0, 0)),
                  pl.BlockSpec((1, N_W), lambda i, j: (0, 0)),
                  pl.BlockSpec((1, C), lambda i, j: (0, 0)),
                  pl.BlockSpec((None, A_W, tn), lambda i, j: (layer, 0, j)),
                  pl.BlockSpec((None, N_W, tn), lambda i, j: (layer, 1, j)),
                  pl.BlockSpec((None, C, tn), lambda i, j: (layer, (A_W + N_W) // C, j))],
        out_specs=pl.BlockSpec((tm, tn), lambda i, j: (i, j)),
        scratch_shapes=[pltpu.VMEM((tm, A_W), BF16), pltpu.VMEM((tm, N_W), BF16), pltpu.VMEM((tm, C), BF16)],
        compiler_params=_cparams(("parallel", "arbitrary")),
        name="out_proj",
    )(oa, on, oc, ga, gn, gc, w_out, w_out, w_out)


def _conv_kernel(cur_ref, prev_ref, w_ref, o_ref, *, C):
    i = pl.program_id(1)
    cur = cur_ref[0]
    u, bg, cg = (cur[:, GATE_W + k * C:GATE_W + (k + 1) * C] for k in range(3))
    z = cg * u
    prev = prev_ref[0]
    zp = prev[:, GATE_W + 2 * C:GATE_W + 3 * C] * prev[:, GATE_W:GATE_W + C]
    zp = jnp.where(i > 0, zp, 0.0)
    rows = lax.broadcasted_iota(jnp.int32, z.shape, 0)
    z1 = jnp.where(rows >= 1, pltpu.roll(z, 1, axis=0), zp[7:8, :])
    z2 = jnp.where(rows >= 2, pltpu.roll(z, 2, axis=0), jnp.where(rows == 1, zp[7:8, :], zp[6:7, :]))
    w = w_ref[...]
    o_ref[0] = bg * (w[0:1, :] * z2 + w[1:2, :] * z1 + w[2:3, :] * z)


def short_conv(pb, conv_w_l, *, tm=512):
    B, S, W = pb.shape
    C = conv_w_l.shape[-1]
    tm = _pick(S, tm)
    return pl.pallas_call(
        functools.partial(_conv_kernel, C=C),
        out_shape=jax.ShapeDtypeStruct((B, S, C), F32),
        grid=(B, S // tm),
        in_specs=[pl.BlockSpec((1, tm, W), lambda b, i: (b, i, 0)),
                  pl.BlockSpec((1, 8, W), lambda b, i: (b, jnp.maximum(i * (tm // 8) - 1, 0), 0)),
                  pl.BlockSpec((CONV_W, C), lambda b, i: (0, 0))],
        out_specs=pl.BlockSpec((1, tm, C), lambda b, i: (b, i, 0)),
        compiler_params=_cparams(("parallel", "parallel")),
        name="short_conv",
    )(pb, pb, conv_w_l)


def _dilated_kernel(sl_ref, q_ref, k_ref, v_ref, o_ref, qf, kf, vf, u_sc, m_sc, l_sc):
    h = pl.program_id(1)
    sb = pl.program_id(2)
    base = sb * A_SUPER

    @pl.when(sb == 0)
    def _():
        kf[...] = k_ref[0].astype(F32)
        vf[...] = v_ref[0].astype(F32)

    qf[...] = q_ref[0].astype(F32)
    slope = sl_ref[h]
    qi = lax.broadcasted_iota(jnp.int32, (A_BLK, A_BLK), 0)
    kj = lax.broadcasted_iota(jnp.int32, (A_BLK, A_BLK), 1)
    diff = (qi - kj).astype(F32)
    cur_ok = kj <= qi
    prev_band = kj >= qi
    nt = (((1,), (1,)), ((), ()))

    for br, (_, d) in enumerate(A_BRANCHES):
        span = A_BLK * d
        sd = slope * float(d)
        bias_c = sd * diff
        bias_p = sd * (diff + float(A_BLK))

        def ds(start):
            return pl.ds(start, A_BLK, stride=d) if d > 1 else pl.ds(start, A_BLK)

        for grp in range(A_SUPER // A_BLK // A_GROUP):
            blocks = []
            for i in range(grp * A_GROUP, (grp + 1) * A_GROUP):
                loc = (i // d) * span + (i % d)
                has_prev = (sb > 0) if loc < span else True
                start = base + loc
                pstart = jnp.where(has_prev, start - span, start) if loc < span else start - span
                blocks.append((ds(loc), ds(start), ds(pstart), has_prev))
            s_c, s_p, hp = [], [], []
            for lrow, rows, prow, has_prev in blocks:
                q = qf[lrow, :].astype(BF16)
                s_c.append(lax.dot_general(q, kf[rows, :].astype(BF16), nt, preferred_element_type=F32))
                s_p.append(lax.dot_general(q, kf[prow, :].astype(BF16), nt, preferred_element_type=F32))
                hp.append(jnp.full((1, 1), jnp.where(has_prev, 0.0, NEG), F32))
            s_c = jnp.where(cur_ok[None], jnp.stack(s_c) - bias_c[None], NEG)
            s_p = jnp.where(prev_band[None], jnp.stack(s_p) - bias_p[None] + jnp.stack(hp), NEG)
            m = jnp.max(jnp.maximum(s_c, s_p), axis=-1, keepdims=True)
            e_c = jnp.exp(s_c - m)
            e_p = jnp.exp(s_p - m)
            l = jnp.sum(e_c + e_p, axis=-1, keepdims=True)
            for n, (lrow, rows, prow, _) in enumerate(blocks):
                u = jnp.dot(e_c[n].astype(BF16), vf[rows, :].astype(BF16), preferred_element_type=F32)
                u += jnp.dot(e_p[n].astype(BF16), vf[prow, :].astype(BF16), preferred_element_type=F32)
                u_sc[br, lrow, :] = u
                m_sc[br, lrow, :] = jnp.broadcast_to(m[n], (A_BLK, HEAD_DIM))
                l_sc[br, lrow, :] = jnp.broadcast_to(l[n], (A_BLK, HEAD_DIM))

    nb = len(A_BRANCHES)
    mt = m_sc[0]
    for br in range(1, nb):
        mt = jnp.maximum(mt, m_sc[br])
    num = jnp.zeros((A_SUPER, HEAD_DIM), F32)
    den = jnp.zeros((A_SUPER, HEAD_DIM), F32)
    for br in range(nb):
        w = jnp.exp(m_sc[br] - mt)
        num += w * u_sc[br]
        den += w * l_sc[br]
    o_ref[0] = num / den


def dilated_attention(pa, slopes):
    B, S, _ = pa.shape
    assert S % A_SUPER == 0
    grid_spec = pltpu.PrefetchScalarGridSpec(
        num_scalar_prefetch=1,
        grid=(B, A_HEADS, S // A_SUPER),
        in_specs=[pl.BlockSpec((1, A_SUPER, HEAD_DIM), lambda b, h, s, sl: (b, s, h)),
                  pl.BlockSpec((1, S, HEAD_DIM), lambda b, h, s, sl: (b, 0, A_HEADS + h)),
                  pl.BlockSpec((1, S, HEAD_DIM), lambda b, h, s, sl: (b, 0, 2 * A_HEADS + h))],
        out_specs=pl.BlockSpec((1, A_SUPER, HEAD_DIM), lambda b, h, s, sl: (b, s, h)),
        scratch_shapes=[pltpu.VMEM((A_SUPER, HEAD_DIM), F32),
                        pltpu.VMEM((S, HEAD_DIM), F32),
                        pltpu.VMEM((S, HEAD_DIM), F32),
                        pltpu.VMEM((len(A_BRANCHES), A_SUPER, HEAD_DIM), F32),
                        pltpu.VMEM((len(A_BRANCHES), A_SUPER, HEAD_DIM), F32),
                        pltpu.VMEM((len(A_BRANCHES), A_SUPER, HEAD_DIM), F32)],
    )
    return pl.pallas_call(
        _dilated_kernel,
        out_shape=jax.ShapeDtypeStruct((B, S, A_W), F32),
        grid_spec=grid_spec,
        compiler_params=_cparams(("parallel", "parallel", "arbitrary")),
        name="dilated_attention",
    )(slopes, pa, pa, pa)


def _compress_kernel(kc_ref, vc_ref, wk_ref, wv_ref, pe_ref, ko_ref, vo_ref, xf, bsc, *, nc):
    for x_ref, w_ref, o_ref in ((kc_ref, wk_ref, ko_ref), (vc_ref, wv_ref, vo_ref)):
        xf[...] = x_ref[0].astype(F32)
        first = jnp.zeros((nc, HEAD_DIM), F32)
        second = jnp.zeros((nc, HEAD_DIM), F32)
        for j in range(CMP_STRIDE):
            xj = xf[pl.ds(j, nc, stride=CMP_STRIDE), :]
            wa = w_ref[j * HEAD_DIM:(j + 1) * HEAD_DIM, :].astype(BF16)
            wb = w_ref[(CMP_STRIDE + j) * HEAD_DIM:(CMP_STRIDE + j + 1) * HEAD_DIM, :].astype(BF16)
            first += jnp.dot((xj + pe_ref[j:j + 1, :]).astype(BF16), wa, preferred_element_type=F32)
            second += jnp.dot((xj + pe_ref[CMP_STRIDE + j:CMP_STRIDE + j + 1, :]).astype(BF16), wb,
                              preferred_element_type=F32)
        bsc[0:nc, :] = second
        bsc[nc:nc + 8, :] = jnp.zeros((8, HEAD_DIM), F32)
        out = first + bsc[1:nc + 1, :]
        row = lax.broadcasted_iota(jnp.int32, out.shape, 0)
        o_ref[0, 0] = jnp.where(row < nc - 1, out, 0.0).astype(o_ref.dtype)


def nsa_compress(pa, cmp_wk, cmp_wv, cmp_pe, layer):
    B, S, _ = pa.shape
    nc = S // CMP_STRIDE
    kc0 = (3 * A_W + N_W) // HEAD_DIM
    vc0 = kc0 + NSA_KV_HEADS
    wspec = pl.BlockSpec((None, CMP_LEN * HEAD_DIM, HEAD_DIM), lambda b, g: (layer, 0, 0))
    ospec = pl.BlockSpec((1, 1, nc, HEAD_DIM), lambda b, g: (b, g, 0, 0))
    oshape = jax.ShapeDtypeStruct((B, NSA_KV_HEADS, nc, HEAD_DIM), BF16)
    return pl.pallas_call(
        functools.partial(_compress_kernel, nc=nc),
        out_shape=[oshape, oshape],
        grid=(B, NSA_KV_HEADS),
        in_specs=[pl.BlockSpec((1, S, HEAD_DIM), lambda b, g: (b, 0, kc0 + g)),
                  pl.BlockSpec((1, S, HEAD_DIM), lambda b, g: (b, 0, vc0 + g)),
                  wspec, wspec,
                  pl.BlockSpec((None, CMP_LEN, HEAD_DIM), lambda b, g: (layer, 0, 0))],
        out_specs=[ospec, ospec],
        scratch_shapes=[pltpu.VMEM((S, HEAD_DIM), F32), pltpu.VMEM((nc + 8, HEAD_DIM), F32)],
        compiler_params=_cparams(("parallel", "parallel")),
        name="nsa_compress",
    )(pa, pa, cmp_wk, cmp_wv, cmp_pe)


def _gate_cols(gt_ref, g, M):
    sig = 1.0 / (1.0 + jnp.exp(-gt_ref[0]))
    lane = lax.broadcasted_iota(jnp.int32, sig.shape, 1)
    return [[jnp.sum(jnp.where(lane == (g * M + m) * 3 + b, sig, 0.0), axis=-1, keepdims=True)
             for b in range(3)] for m in range(M)]


def _nsa_select_kernel(sl_ref, q_ref, gt_ref, kc_ref, vc_ref, op_ref, mn_ref, *, TQ, nc):
    g = pl.program_id(1)
    q0 = pl.program_id(2) * TQ
    M = NSA_GROUP
    nt = (((1,), (1,)), ((), ()))

    q = q_ref[0]
    qs = jnp.concatenate([q[:, m * HEAD_DIM:(m + 1) * HEAD_DIM] for m in range(M)], axis=0)
    slope_col = jnp.concatenate([jnp.full((TQ, 1), sl_ref[g * M + m], F32) for m in range(M)], axis=0)
    tq_i = q0 + lax.broadcasted_iota(jnp.int32, (TQ, 1), 0)
    t_col = jnp.concatenate([tq_i] * M, axis=0).astype(F32)

    def softmax_rows(s, ok):
        s = jnp.where(ok, s, NEG)
        m = jnp.max(s, axis=-1, keepdims=True)
        e = jnp.where(ok, jnp.exp(s - m), 0.0)
        den = jnp.sum(e, axis=-1, keepdims=True)
        return e / jnp.maximum(den, TINY)

    cidx = lax.broadcasted_iota(jnp.int32, (1, nc), 1)
    c_end = (cidx * CMP_STRIDE + (CMP_LEN - 1)).astype(F32)
    dist = t_col - c_end
    s = lax.dot_general(qs, kc_ref[0, 0], nt, preferred_element_type=F32) - slope_col * dist
    p_cmp = softmax_rows(s, (dist >= 0) & (cidx < nc - 1))
    o_cmp = jnp.dot(p_cmp.astype(BF16), vc_ref[0, 0], preferred_element_type=F32)

    psum = p_cmp[0:TQ]
    for m in range(1, M):
        psum = psum + p_cmp[m * TQ:(m + 1) * TQ]
    ci = lax.broadcasted_iota(jnp.int32, (nc, LANES), 0)
    ni = lax.broadcasted_iota(jnp.int32, (nc, LANES), 1)
    overlap = ((ci * CMP_STRIDE < ni * SEL_BLK + SEL_BLK) & (ci * CMP_STRIDE + CMP_LEN - 1 >= ni * SEL_BLK)
               & (ci < nc - 1))
    overlap = jnp.where(overlap, 1.0, 0.0).astype(BF16)
    imp = jnp.zeros((TQ, LANES), F32)
    for piece in _split3(psum):
        imp = imp + jnp.dot(piece, overlap, preferred_element_type=F32)
    jb = lax.broadcasted_iota(jnp.int32, (TQ, LANES), 1)
    cur = tq_i // SEL_BLK
    forced = (jb == 0) | (jb == cur) | (jb == cur - 1)
    imp = jnp.where(forced, imp + FORCE, imp)
    causal_blk = jb * SEL_BLK <= tq_i
    imp = jnp.where(causal_blk, imp, NEG)

    work = imp
    memb = jnp.zeros((TQ, LANES), F32)
    jbf = jb.astype(F32)
    for _ in range(TOP_N):
        mx = jnp.max(work, axis=-1, keepdims=True)
        first = jnp.min(jnp.where(work == mx, jbf, float(LANES)), axis=-1, keepdims=True)
        pick = jbf == first
        memb = jnp.where(pick, 1.0, memb)
        work = jnp.where(pick, LOWEST, work)
    chosen = (memb > 0.5) & causal_blk
    mn_ref[0, 0] = jnp.where(chosen, 0.0, NEG).astype(mn_ref.dtype)
    gates = _gate_cols(gt_ref, g, M)
    for m in range(M):
        op_ref[0, :, m * HEAD_DIM:(m + 1) * HEAD_DIM] = gates[m][0] * o_cmp[m * TQ:(m + 1) * TQ]


def nsa_select(pa, pb, k_cmp, v_cmp, slopes, *, TQ=512):
    B, S, _ = pa.shape
    nc = k_cmp.shape[2]
    M = NSA_GROUP
    TQ = _pick(S, TQ)
    assert S // SEL_BLK <= LANES
    q0 = 3 * A_W // (M * HEAD_DIM)
    cmp_spec = pl.BlockSpec((1, 1, nc, HEAD_DIM), lambda b, g, t, sl: (b, g, 0, 0))
    grid_spec = pltpu.PrefetchScalarGridSpec(
        num_scalar_prefetch=1,
        grid=(B, NSA_KV_HEADS, S // TQ),
        in_specs=[pl.BlockSpec((1, TQ, M * HEAD_DIM), lambda b, g, t, sl: (b, t, q0 + g)),
                  pl.BlockSpec((1, TQ, LANES), lambda b, g, t, sl: (b, t, 0)),
                  cmp_spec, cmp_spec],
        out_specs=[pl.BlockSpec((1, TQ, M * HEAD_DIM), lambda b, g, t, sl: (b, t, g)),
                   pl.BlockSpec((1, 1, TQ, LANES), lambda b, g, t, sl: (b, g, t, 0))],
    )
    return pl.pallas_call(
        functools.partial(_nsa_select_kernel, TQ=TQ, nc=nc),
        out_shape=[jax.ShapeDtypeStruct((B, S, N_W), F32),
                   jax.ShapeDtypeStruct((B, NSA_KV_HEADS, S, LANES), BF16)],
        grid_spec=grid_spec,
        compiler_params=_cparams(("parallel", "parallel", "parallel")),
        name="nsa_select",
    )(slopes, pa, pb, k_cmp, v_cmp)


def _nsa_sweep_kernel(sl_ref, q_ref, gt_ref, op_ref, mn_ref, ks_ref, vs_ref, kw_ref, vw_ref, o_ref,
                      qa_sc, mx_sc, l_sc, acc_sc, flag_ref, act_ref, *, TQ, n_kt, G):
    g = pl.program_id(1)
    qt = pl.program_id(2)
    M = NSA_GROUP
    R = M * TQ
    kt = TQ
    bpk = kt // SEL_BLK
    nt = (((1,), (1,)), ((), ()))

    q = q_ref[0]
    mn = mn_ref[0, 0]
    for m in range(M):
        qa_sc[m * TQ:(m + 1) * TQ, 0:HEAD_DIM] = q[:, m * HEAD_DIM:(m + 1) * HEAD_DIM]
        qa_sc[m * TQ:(m + 1) * TQ, HEAD_DIM:2 * HEAD_DIM] = mn
    slope_rep = jnp.concatenate([jnp.full((TQ, kt), sl_ref[g * M + m], F32) for m in range(M)], axis=0)
    trel = jnp.concatenate([lax.broadcasted_iota(jnp.int32, (TQ, 1), 0)] * M, axis=0)
    lane = lax.broadcasted_iota(jnp.int32, (1, kt), 1)

    cnt = jnp.sum(jnp.where(mn.astype(F32) == 0.0, 1.0, 0.0), axis=0, keepdims=True)
    cnt = jnp.broadcast_to(cnt, (8, LANES))
    per_tile = cnt
    for b in range(1, bpk):
        per_tile = per_tile + pltpu.roll(cnt, LANES - b, axis=1)
    for j in range(n_kt):
        flag_ref[j] = (per_tile[0, j * bpk] > 0.0).astype(jnp.int32)

    act_ref[0] = 0

    def compact(j, n):
        act_ref[n] = j
        return n + flag_ref[j]

    n_act = lax.fori_loop(0, qt, compact, 0)
    for u in range(1, G):
        act_ref[n_act + u - 1] = act_ref[0]
    n_grp = (n_act + G - 1) // G

    def rows_of(j):
        return pl.ds(pl.multiple_of(j * kt, kt), kt)

    def sel_scores(j, shift):
        ki = lax.broadcasted_iota(jnp.int32, (kt, LANES), 0)
        bi = lax.broadcasted_iota(jnp.int32, (kt, LANES), 1)
        onehot = jnp.where(bi == j * bpk + ki // SEL_BLK, 1.0, 0.0).astype(BF16)
        s = lax.dot_general(qa_sc[...], jnp.concatenate([ks_ref[0, rows_of(j), :], onehot], axis=1), nt,
                            preferred_element_type=F32)
        return s + slope_rep * (((j - qt) * kt + lane).astype(F32) + shift)

    mx_sc[...] = jnp.full((R, kt), 0.5 * NEG, F32)

    def max_body(gi, carry):
        m = mx_sc[...]
        for u in range(G):
            m = jnp.maximum(m, sel_scores(act_ref[gi * G + u], 0.0))
        mx_sc[...] = m
        return carry

    lax.fori_loop(0, n_grp, max_body, 0)
    s_diag = jnp.where(lane <= trel, sel_scores(qt, 0.0), NEG)
    m_row = jnp.max(jnp.maximum(mx_sc[...], s_diag), axis=-1, keepdims=True)
    mx_sc[...] = jnp.broadcast_to(m_row, (R, kt))

    p = jnp.exp(s_diag - m_row)
    l_sc[...] = p
    acc_sc[...] = jnp.dot(p.astype(BF16), vs_ref[0, rows_of(qt), :], preferred_element_type=F32)

    def acc_body(gi, carry):
        m = mx_sc[...]
        l = l_sc[...]
        ps, vs = [], []
        for u in range(G):
            idx = gi * G + u
            j = act_ref[idx]
            p = jnp.exp(sel_scores(j, jnp.where(idx < n_act, 0.0, -1e34)) - m)
            l = l + p
            ps.append(p.astype(BF16))
            vs.append(vs_ref[0, rows_of(j), :])
        l_sc[...] = l
        acc_sc[...] += jnp.dot(jnp.concatenate(ps, axis=1), jnp.concatenate(vs, axis=0), preferred_element_type=F32)
        return carry

    lax.fori_loop(0, n_grp, acc_body, 0)
    o_sel = acc_sc[...] / jnp.maximum(jnp.sum(l_sc[...], axis=-1, keepdims=True), TINY)

    n_w = NSA_WIN // kt + 1
    s_w, v_w = [], []
    for u in range(n_w):
        jc = jnp.maximum(qt - (n_w - 1) + u, 0)
        prel = (u - (n_w - 1)) * kt + lane
        d = trel - prel
        ok = (d >= 0) & (d < NSA_WIN) & (prel >= -qt * kt)
        s = lax.dot_general(qa_sc[:, 0:HEAD_DIM], kw_ref[0, rows_of(jc), :], nt, preferred_element_type=F32)
        s_w.append(jnp.where(ok, s + slope_rep * prel.astype(F32), NEG))
        v_w.append(vw_ref[0, rows_of(jc), :])
    m_el = s_w[0]
    for s in s_w[1:]:
        m_el = jnp.maximum(m_el, s)
    m_row = jnp.max(m_el, axis=-1, keepdims=True)
    p_w = [jnp.exp(s - m_row) for s in s_w]
    l_el = p_w[0]
    for p in p_w[1:]:
        l_el = l_el + p
    o_win = jnp.dot(jnp.concatenate([p.astype(BF16) for p in p_w], axis=1), jnp.concatenate(v_w, axis=0),
                    preferred_element_type=F32) / jnp.maximum(jnp.sum(l_el, axis=-1, keepdims=True), TINY)

    gates = _gate_cols(gt_ref, g, M)
    for m in range(M):
        rs = slice(m * TQ, (m + 1) * TQ)
        cs = slice(m * HEAD_DIM, (m + 1) * HEAD_DIM)
        o_ref[0, :, cs] = op_ref[0, :, cs] + gates[m][1] * o_sel[rs] + gates[m][2] * o_win[rs]


def nsa_sweep(pa, pb, o_part, mneg, slopes, *, TQ=128, group=4):
    B, S, _ = pa.shape
    M = NSA_GROUP
    assert TQ == LANES and TQ % SEL_BLK == 0 and NSA_WIN % TQ == 0 and S % TQ == 0
    q0 = 3 * A_W // (M * HEAD_DIM)
    kv0 = (3 * A_W + N_W) // HEAD_DIM + 2 * NSA_KV_HEADS
    kv = lambda n: pl.BlockSpec((1, S, HEAD_DIM), lambda b, g, t, sl: (b, 0, kv0 + n * NSA_KV_HEADS + g))
    qspec = pl.BlockSpec((1, TQ, M * HEAD_DIM), lambda b, g, t, sl: (b, t, q0 + g))
    ospec = pl.BlockSpec((1, TQ, M * HEAD_DIM), lambda b, g, t, sl: (b, t, g))
    grid_spec = pltpu.PrefetchScalarGridSpec(
        num_scalar_prefetch=1,
        grid=(B, NSA_KV_HEADS, S // TQ),
        in_specs=[qspec,
                  pl.BlockSpec((1, TQ, LANES), lambda b, g, t, sl: (b, t, 0)),
                  ospec,
                  pl.BlockSpec((1, 1, TQ, LANES), lambda b, g, t, sl: (b, g, t, 0)),
                  kv(0), kv(1), kv(2), kv(3)],
        out_specs=ospec,
        scratch_shapes=[pltpu.VMEM((M * TQ, 2 * HEAD_DIM), BF16),
                        pltpu.VMEM((M * TQ, TQ), F32), pltpu.VMEM((M * TQ, TQ), F32),
                        pltpu.VMEM((M * TQ, HEAD_DIM), F32),
                        pltpu.SMEM((S // TQ,), jnp.int32),
                        pltpu.SMEM((S // TQ + group,), jnp.int32)],
    )
    return pl.pallas_call(
        functools.partial(_nsa_sweep_kernel, TQ=TQ, n_kt=S // TQ, G=group),
        out_shape=jax.ShapeDtypeStruct((B, S, N_W), F32),
        grid_spec=grid_spec,
        compiler_params=_cparams(("parallel", "parallel", "arbitrary")),
        name="nsa_sweep",
    )(slopes, pa, pb, o_part, mneg, pa, pa, pa, pa)


def kernel(x, c, ada_w, ada_b, w_in, conv_w, cmp_wk, cmp_wv, cmp_pe, mix_g, w_out, ln1_g, ln1_b, ln2_g, ln2_b,
           ffn_w_gate, ffn_w_up, ffn_w_down, moe_router, moe_router_b, moe_w_gate, moe_w_up, moe_w_down):
    B, S, D = x.shape
    depth = ada_w.shape[0]
    C = conv_w.shape[-1]
    N = B * S
    alpha = (2 * depth) ** 0.25
    assert D == A_W + N_W + C and w_in.shape[-1] == ATT_W + GATE_W + 3 * C and ATT_W % PB_TN == 0
    pb_w = pl.cdiv(GATE_W + 3 * C, PB_TN) * PB_TN

    n_heads = A_HEADS + NSA_HEADS
    sl = jnp.exp2(-8.0 * jnp.arange(1, n_heads + 1, dtype=F32) / n_heads)
    sl_a, sl_n = sl[0::2], sl[1::2]
    scale = HEAD_DIM ** -0.5
    col = jnp.arange(ATT_W)
    is_q = (col < A_W) | ((col >= 3 * A_W) & (col < 3 * A_W + N_W))
    q_scale = jnp.where(is_q, scale, 1.0).astype(F32)[None, :]

    cond = jnp.zeros((8, D), F32).at[:B].set(c * jax.nn.sigmoid(c)).astype(BF16)
    mods = [matmul(cond, ada_w, i, 6 * D, tm=8, tn=1024)[:B] + ada_b[i] for i in range(depth)]
    h = None
    gate = None
    for i in range(depth):
        sh1, sc1, g1, sh2, sc2, g2 = [m[:, None, :] for m in jnp.split(mods[i], 6, axis=-1)]
        if i == 0:
            h = ln_mod(x, sc1, sh1)

        hf = h.reshape(N, D)
        pa = matmul(hf, w_in, i, ATT_W, scale=q_scale, out_dtype=BF16, tn=256).reshape(B, S, ATT_W)
        pb = matmul(hf, w_in, i, pb_w, col0=ATT_W, tn=PB_TN).reshape(B, S, pb_w)
        oa = dilated_attention(pa, sl_a)
        k_cmp, v_cmp = nsa_compress(pa, cmp_wk, cmp_wv, cmp_pe, i)
        o_part, mneg = nsa_select(pa, pb, k_cmp, v_cmp, sl_n)
        on = nsa_sweep(pa, pb, o_part, mneg, sl_n)
        oc = short_conv(pb, conv_w[i])
        y = out_proj(oa.reshape(N, A_W), on.reshape(N, N_W), oc.reshape(N, C), mix_g, w_out, i).reshape(B, S, D)

        moe_layer = i % 2 == 1
        j = i // 2
        if moe_layer:
            rw = jnp.zeros((D, LANES), F32).at[:, :N_EXPERTS].set(moe_router[j])
            rb = jnp.zeros((1, LANES), F32).at[0, :N_EXPERTS].set(moe_router_b[j])
            x, h, table = res_ln(x, y, g1, ln1_g[i:i + 1], ln1_b[i:i + 1], alpha, sc=sc2, sh=sh2,
                                 router_w=rw, router_b=rb)
        else:
            x, h = res_ln(x, y, g1, ln1_g[i:i + 1], ln1_b[i:i + 1], alpha, sc=sc2, sh=sh2)

        hf = h.reshape(N, D)
        if moe_layer:
            TM = _pick(N, 1024)
            plan = moe_plan(table.reshape(N, LANES), TM)
            a = moe_gate_up(hf, moe_w_gate, moe_w_up, j, plan, TM=TM, chunk=_pick(TM, 512))
            ye = moe_down(a, moe_w_down, j, plan, TM=TM)
            y = moe_combine(ye, plan, N)
        else:
            a = gate_up(hf, ffn_w_gate, ffn_w_up, (j,))
            y = matmul(a, ffn_w_down, j, D, tm=2048, tn=1024, tk=768)
        y = y.reshape(B, S, D)

        if i + 1 < depth:
            sh_n, sc_n = mods[i + 1][:, None, :D], mods[i + 1][:, None, D:2 * D]
            x, h = res_ln(x, y, g2, ln2_g[i:i + 1], ln2_b[i:i + 1], alpha, sc=sc_n, sh=sh_n)
        else:
            (x,) = res_ln(x, y, g2, ln2_g[i:i + 1], ln2_b[i:i + 1], alpha)
    return x
```

```python
import functools

import jax
import jax.numpy as jnp
from jax import lax
from jax.experimental import pallas as pl
from jax.experimental.pallas import tpu as pltpu

F32 = jnp.float32
BF16 = jnp.bfloat16

HEAD_DIM = 128
A_HEADS = 12
A_BRANCHES = ((128, 1), (512, 4), (2048, 16))
A_BLK = 128
A_SUPER = A_BLK * max(d for _, d in A_BRANCHES)
A_GROUP = 8
NSA_HEADS = 12
NSA_KV_HEADS = 3
NSA_GROUP = NSA_HEADS // NSA_KV_HEADS
CMP_LEN = 32
CMP_STRIDE = 16
SEL_BLK = 64
TOP_N = 16
NSA_WIN = 512
CONV_W = 3
N_EXPERTS = 8
LN_EPS = 1e-5
NEG = -1e30
TINY = 1e-30
FORCE = 1e4
LOWEST = -3e38

A_W = A_HEADS * HEAD_DIM
N_W = NSA_HEADS * HEAD_DIM
KV_W = NSA_KV_HEADS * HEAD_DIM
ATT_W = 3 * A_W + N_W + 6 * KV_W
GATE_W = NSA_HEADS * 3
LANES = 128
PB_TN = 2 * LANES
VMEM_LIMIT = 56 * 1024 * 1024


def _cparams(sem):
    return pltpu.CompilerParams(dimension_semantics=sem, vmem_limit_bytes=VMEM_LIMIT)


def _pick(n, pref):
    if n <= pref:
        return n
    t = pref
    while n % t:
        t //= 2
    return t


def _mm_kernel(*refs, nk, k_tail, has_scale):
    if has_scale:
        x_ref, w_ref, s_ref, o_ref, acc_ref = refs
    else:
        x_ref, w_ref, o_ref, acc_ref = refs
        s_ref = None

    def product(tail):
        x = x_ref[...]
        w = w_ref[...]
        if tail:
            x = jnp.where(lax.broadcasted_iota(jnp.int32, x.shape, 1) < tail, x, jnp.zeros_like(x))
            w = jnp.where(lax.broadcasted_iota(jnp.int32, w.shape, 0) < tail, w, jnp.zeros_like(w))
        return jnp.dot(x, w.astype(BF16), preferred_element_type=F32)

    def finish(acc):
        if s_ref is not None:
            acc = acc * s_ref[...]
        o_ref[...] = acc.astype(o_ref.dtype)

    if nk == 1:
        finish(product(k_tail))
        return
    k = pl.program_id(2)

    @pl.when(k == 0)
    def _():
        acc_ref[...] = product(0)

    @pl.when((k > 0) & (k < nk - 1))
    def _():
        acc_ref[...] += product(0)

    @pl.when(k == nk - 1)
    def _():
        finish(acc_ref[...] + product(k_tail))


def matmul(x, w, w_idx, n_cols, *, col0=0, scale=None, out_dtype=F32, tm=1024, tn=512, tk=None):
    M, K = x.shape
    tm = _pick(M, tm)
    tn = _pick(n_cols, tn)
    tk = K if tk is None or tk >= K else tk
    assert col0 % tn == 0 and col0 + n_cols <= w.shape[-1]
    nk = pl.cdiv(K, tk)
    cb = col0 // tn
    lead = (w_idx,) if w.ndim == 3 else ()
    wblk = ((None,) if w.ndim == 3 else ()) + (tk, tn)
    in_specs = [pl.BlockSpec((tm, tk), lambda i, j, k: (i, k)),
                pl.BlockSpec(wblk, lambda i, j, k: lead + (k, cb + j))]
    args = [x, w]
    if scale is not None:
        in_specs.append(pl.BlockSpec((1, tn), lambda i, j, k: (0, j)))
        args.append(scale)
    return pl.pallas_call(
        functools.partial(_mm_kernel, nk=nk, k_tail=K % tk, has_scale=scale is not None),
        out_shape=jax.ShapeDtypeStruct((M, n_cols), out_dtype),
        grid=(M // tm, n_cols // tn, nk),
        in_specs=in_specs,
        out_specs=pl.BlockSpec((tm, tn), lambda i, j, k: (i, j)),
        scratch_shapes=[pltpu.VMEM((tm, tn), F32)],
        compiler_params=_cparams(("parallel", "parallel", "arbitrary")),
        name="matmul",
    )(*args)


def _repack_kernel(wt_ref, o_ref, *, n_valid):
    tn = wt_ref.shape[0]
    col = pl.program_id(0) * tn + lax.broadcasted_iota(jnp.int32, (1, tn), 1)
    for layer in range(wt_ref.shape[1]):
        w = jnp.transpose(wt_ref[:, layer, :]).astype(BF16)
        o_ref[layer] = jnp.where(col < n_valid, w, jnp.zeros_like(w))


def repack_weight(wt, n_cols, *, tn=256):
    N, L, K = wt.shape
    assert n_cols % tn == 0 and n_cols >= N
    return pl.pallas_call(
        functools.partial(_repack_kernel, n_valid=N),
        out_shape=jax.ShapeDtypeStruct((L, K, n_cols), BF16),
        grid=(n_cols // tn,),
        in_specs=[pl.BlockSpec((tn, L, K), lambda j: (j, 0, 0))],
        out_specs=pl.BlockSpec((L, K, tn), lambda j: (0, 0, j)),
        compiler_params=_cparams(("parallel",)),
        name="repack_weight",
    )(wt)


def _gate_up_kernel(x_ref, wg_ref, wu_ref, o_ref):
    x = x_ref[...]
    g = jnp.dot(x, wg_ref[...].astype(BF16), preferred_element_type=F32)
    u = jnp.dot(x, wu_ref[...].astype(BF16), preferred_element_type=F32)
    o_ref[...] = (g * (1.0 / (1.0 + jnp.exp(-g))) * u).astype(o_ref.dtype)


def gate_up(x, wg, wu, lead, *, tm=1024, tn=256):
    M, K = x.shape
    F = wg.shape[-1]
    tm = _pick(M, tm)
    tn = _pick(F, tn)
    nlead = len(lead)
    wspec = pl.BlockSpec((None,) * nlead + (K, tn), lambda i, j: lead + (0, j))
    return pl.pallas_call(
        _gate_up_kernel,
        out_shape=jax.ShapeDtypeStruct((M, F), BF16),
        grid=(M // tm, F // tn),
        in_specs=[pl.BlockSpec((tm, K), lambda i, j: (i, 0)), wspec, wspec],
        out_specs=pl.BlockSpec((tm, tn), lambda i, j: (i, j)),
        compiler_params=_cparams(("parallel", "parallel")),
        name="gate_up",
    )(x, wg, wu)


def moe_plan(table, tile_rows):
    N = table.shape[0]
    e = jnp.concatenate([table[:, 0], table[:, 1]]).astype(jnp.int32)
    w = jnp.concatenate([table[:, 2], table[:, 3]])
    onehot = (e[:, None] == jnp.arange(N_EXPERTS)[None, :]).astype(jnp.int32)
    csum = jnp.cumsum(onehot, axis=0)
    rank = jnp.sum(csum * onehot, axis=1) - 1
    tiles = (csum[-1] + tile_rows - 1) // tile_rows
    tile_end = jnp.cumsum(tiles)
    row = (tile_end - tiles)[e] * tile_rows + rank
    T = 2 * N // tile_rows + N_EXPERTS
    tok = (jnp.arange(2 * N, dtype=jnp.int32) % N).astype(F32)
    both = jnp.zeros((T * tile_rows, 2), F32).at[row].set(jnp.stack([tok, w], axis=1))
    src = both[:, 0].astype(jnp.int32)
    row_gate = both[:, 1]
    n_used = tile_end[-1:].astype(jnp.int32)
    t_idx = jnp.minimum(jnp.arange(T), n_used[0] - 1)
    tile_expert = jnp.minimum(jnp.searchsorted(tile_end, t_idx, side="right"), N_EXPERTS - 1).astype(jnp.int32)
    return dict(src=src.reshape(T, 1, tile_rows), row_gate=row_gate.reshape(T * tile_rows, 1),
                tile_expert=tile_expert, n_used=n_used, rows=jnp.stack([row[:N], row[N:]], axis=0), T=T)


def _row_copy(src_hbm, row, dst, slot, sem):
    return pltpu.make_async_copy(src_hbm.at[pl.ds(row, 1), :], dst.at[pl.ds(slot, 1), :], sem)


def _moe_gate_up_kernel(te_ref, nu_ref, src_ref, h_hbm, wg_ref, wu_ref, o_ref, xf, xb, sem, *, TM, chunk):
    t = pl.program_id(0)
    used = t < nu_ref[0]

    @pl.when(used & (pl.program_id(1) == 0))
    def _():
        for part in range(TM // chunk):
            def issue(r, carry, part=part):
                _row_copy(h_hbm, src_ref[0, 0, part * chunk + r], xf, r, sem).start()
                return carry

            lax.fori_loop(0, chunk, issue, 0, unroll=8)
            pltpu.make_async_copy(h_hbm.at[pl.ds(0, chunk), :], xf, sem).wait()
            xb[part * chunk:(part + 1) * chunk, :] = xf[...].astype(BF16)

    @pl.when(used)
    def _():
        x = xb[...]
        g = jnp.dot(x, wg_ref[...].astype(BF16), preferred_element_type=F32)
        u = jnp.dot(x, wu_ref[...].astype(BF16), preferred_element_type=F32)
        o_ref[...] = (g * (1.0 / (1.0 + jnp.exp(-g))) * u).astype(o_ref.dtype)

    @pl.when(jnp.logical_not(used))
    def _():
        o_ref[...] = jnp.zeros_like(o_ref)


def moe_gate_up(h, wg, wu, layer, plan, *, TM, tn=256, chunk=512):
    N, K = h.shape
    F = wg.shape[-1]
    T = plan["T"]
    tn = _pick(F, tn)
    nj = F // tn
    wspec = pl.BlockSpec((None, None, K, tn),
                         lambda t, j, te, nu: (layer, te[t], 0, jnp.where(t < nu[0], j, nj - 1)))
    grid_spec = pltpu.PrefetchScalarGridSpec(
        num_scalar_prefetch=2,
        grid=(T, nj),
        in_specs=[pl.BlockSpec((1, 1, TM), lambda t, j, te, nu: (t, 0, 0), memory_space=pltpu.SMEM),
                  pl.BlockSpec(memory_space=pl.ANY), wspec, wspec],
        out_specs=pl.BlockSpec((TM, tn), lambda t, j, te, nu: (t, j)),
        scratch_shapes=[pltpu.VMEM((chunk, K), F32), pltpu.VMEM((TM, K), BF16), pltpu.SemaphoreType.DMA],
    )
    return pl.pallas_call(
        functools.partial(_moe_gate_up_kernel, TM=TM, chunk=chunk),
        out_shape=jax.ShapeDtypeStruct((T * TM, F), BF16),
        grid_spec=grid_spec,
        compiler_params=_cparams(("arbitrary", "arbitrary")),
        name="moe_gate_up",
    )(plan["tile_expert"], plan["n_used"], plan["src"], h, wg, wu)


def _moe_down_kernel(te_ref, nu_ref, a_ref, w_ref, rg_ref, o_ref):
    used = pl.program_id(0) < nu_ref[0]

    @pl.when(used)
    def _():
        o_ref[...] = rg_ref[...] * jnp.dot(a_ref[...], w_ref[...].astype(BF16), preferred_element_type=F32)

    @pl.when(jnp.logical_not(used))
    def _():
        o_ref[...] = jnp.zeros_like(o_ref)


def moe_down(a, wd, layer, plan, *, TM, tn=512):
    P, F = a.shape
    D = wd.shape[-1]
    T = plan["T"]
    tn = _pick(D, tn)
    nj = D // tn
    grid_spec = pltpu.PrefetchScalarGridSpec(
        num_scalar_prefetch=2,
        grid=(T, nj),
        in_specs=[pl.BlockSpec((TM, F), lambda t, j, te, nu: (t, 0)),
                  pl.BlockSpec((None, None, F, tn),
                               lambda t, j, te, nu: (layer, te[t], 0, jnp.where(t < nu[0], j, nj - 1))),
                  pl.BlockSpec((TM, 1), lambda t, j, te, nu: (t, 0))],
        out_specs=pl.BlockSpec((TM, tn), lambda t, j, te, nu: (t, j)),
    )
    return pl.pallas_call(
        _moe_down_kernel,
        out_shape=jax.ShapeDtypeStruct((P, D), F32),
        grid_spec=grid_spec,
        compiler_params=_cparams(("arbitrary", "arbitrary")),
        name="moe_down",
    )(plan["tile_expert"], plan["n_used"], a, wd, plan["row_gate"])


def _moe_combine_kernel(rows_ref, ye_hbm, o_ref, buf, sem, *, tm):
    def issue(r, carry):
        for c in range(2):
            _row_copy(ye_hbm, rows_ref[0, c, r], buf, c * tm + r, sem).start()
        return carry

    lax.fori_loop(0, tm, issue, 0, unroll=8)
    pltpu.make_async_copy(ye_hbm.at[pl.ds(0, 2 * tm), :], buf, sem).wait()
    o_ref[...] = buf[0:tm, :] + buf[tm:2 * tm, :]


def moe_combine(ye, plan, N, *, tm=256):
    D = ye.shape[1]
    tm = _pick(N, tm)
    rows = plan["rows"].reshape(2, N // tm, tm).transpose(1, 0, 2)
    return pl.pallas_call(
        functools.partial(_moe_combine_kernel, tm=tm),
        out_shape=jax.ShapeDtypeStruct((N, D), F32),
        grid=(N // tm,),
        in_specs=[pl.BlockSpec((1, 2, tm), lambda i: (i, 0, 0), memory_space=pltpu.SMEM),
                  pl.BlockSpec(memory_space=pl.ANY)],
        out_specs=pl.BlockSpec((tm, D), lambda i: (i, 0)),
        scratch_shapes=[pltpu.VMEM((2 * tm, D), F32), pltpu.SemaphoreType.DMA],
        compiler_params=_cparams(("arbitrary",)),
        name="moe_combine",
    )(rows, ye)


def _ln(v):
    mu = jnp.mean(v, axis=-1, keepdims=True)
    c = v - mu
    var = jnp.mean(c * c, axis=-1, keepdims=True)
    return c * lax.rsqrt(var + LN_EPS)


def _ln_mod_kernel(x_ref, sc_ref, sh_ref, h_ref):
    h_ref[0] = (_ln(x_ref[0]) * (1.0 + sc_ref[0]) + sh_ref[0]).astype(h_ref.dtype)


def ln_mod(x, sc, sh, *, tm=256):
    B, S, D = x.shape
    tm = _pick(S, tm)
    row = pl.BlockSpec((1, tm, D), lambda b, i: (b, i, 0))
    vec = pl.BlockSpec((1, 1, D), lambda b, i: (b, 0, 0))
    return pl.pallas_call(
        _ln_mod_kernel,
        out_shape=jax.ShapeDtypeStruct((B, S, D), BF16),
        grid=(B, S // tm),
        in_specs=[row, vec, vec],
        out_specs=row,
        compiler_params=_cparams(("parallel", "parallel")),
        name="ln_mod",
    )(x, sc, sh)


def _split3(v):
    hi = v.astype(BF16)
    r1 = v - hi.astype(F32)
    mid = r1.astype(BF16)
    lo = (r1 - mid.astype(F32)).astype(BF16)
    return hi, mid, lo


def _res_ln_kernel(*refs, alpha, with_h, with_router):
    it = iter(refs)
    x_ref, y_ref, g_ref, lng_ref, lnb_ref = (next(it) for _ in range(5))
    sc_ref = sh_ref = rw_ref = rb_ref = None
    if with_h:
        sc_ref, sh_ref = next(it), next(it)
    if with_router:
        rw_ref, rb_ref = next(it), next(it)
    xo_ref = next(it)
    h_ref = next(it) if with_h else None
    gate_ref = next(it) if with_router else None

    v = alpha * x_ref[0] + (1.0 + g_ref[0]) * y_ref[0]
    xn = _ln(v) * lng_ref[...] + lnb_ref[...]
    xo_ref[0] = xn
    if not with_h:
        return
    h = _ln(xn) * (1.0 + sc_ref[0]) + sh_ref[0]
    h_ref[0] = h.astype(h_ref.dtype)
    if not with_router:
        return
    hs = _split3(h)
    ws = _split3(rw_ref[...])
    logits = rb_ref[...]
    for a in range(3):
        for b in range(3 - a):
            logits = logits + jnp.dot(hs[a], ws[b], preferred_element_type=F32)
    lane = lax.broadcasted_iota(jnp.int32, logits.shape, 1).astype(F32)
    logits = jnp.where(lane < N_EXPERTS, logits, LOWEST)
    v1 = jnp.max(logits, axis=-1, keepdims=True)
    i1 = jnp.min(jnp.where(logits == v1, lane, float(LANES)), axis=-1, keepdims=True)
    rest = jnp.where(lane == i1, LOWEST, logits)
    v2 = jnp.max(rest, axis=-1, keepdims=True)
    i2 = jnp.min(jnp.where(rest == v2, lane, float(LANES)), axis=-1, keepdims=True)
    e2 = jnp.exp(v2 - v1)
    den = 1.0 + e2
    gate_ref[0] = jnp.where(lane == 0.0, i1, jnp.where(lane == 1.0, i2,
                            jnp.where(lane == 2.0, 1.0 / den, jnp.where(lane == 3.0, e2 / den, 0.0))))


def res_ln(x, y, g, lng, lnb, alpha, *, sc=None, sh=None, router_w=None, router_b=None, tm=256):
    B, S, D = x.shape
    tm = _pick(S, tm)
    with_h = sc is not None
    with_router = router_w is not None
    row = pl.BlockSpec((1, tm, D), lambda b, i: (b, i, 0))
    vec = pl.BlockSpec((1, 1, D), lambda b, i: (b, 0, 0))
    par = pl.BlockSpec((1, D), lambda b, i: (0, 0))
    in_specs = [row, row, vec, par, par]
    args = [x, y, g, lng, lnb]
    out_shape = [jax.ShapeDtypeStruct((B, S, D), F32)]
    out_specs = [row]
    if with_h:
        in_specs += [vec, vec]
        args += [sc, sh]
        out_shape.append(jax.ShapeDtypeStruct((B, S, D), F32 if with_router else BF16))
        out_specs.append(row)
    if with_router:
        in_specs += [pl.BlockSpec((D, LANES), lambda b, i: (0, 0)), pl.BlockSpec((1, LANES), lambda b, i: (0, 0))]
        args += [router_w, router_b]
        out_shape.append(jax.ShapeDtypeStruct((B, S, LANES), F32))
        out_specs.append(pl.BlockSpec((1, tm, LANES), lambda b, i: (b, i, 0)))
    return pl.pallas_call(
        functools.partial(_res_ln_kernel, alpha=alpha, with_h=with_h, with_router=with_router),
        out_shape=out_shape,
        grid=(B, S // tm),
        in_specs=in_specs,
        out_specs=out_specs,
        compiler_params=_cparams(("parallel", "parallel")),
        name="res_ln",
    )(*args)


def _rms(v, g):
    return (v * lax.rsqrt(jnp.mean(v * v, axis=-1, keepdims=True) + LN_EPS) * g).astype(BF16)


def _out_proj_kernel(oa_ref, on_ref, oc_ref, ga_ref, gn_ref, gc_ref, wa_ref, wn_ref, wc_ref, o_ref,
                     ya_sc, yn_sc, yc_sc):
    @pl.when(pl.program_id(1) == 0)
    def _():
        ya_sc[...] = _rms(oa_ref[...], ga_ref[...])
        yn_sc[...] = _rms(on_ref[...], gn_ref[...])
        yc_sc[...] = _rms(oc_ref[...], gc_ref[...])

    acc = jnp.dot(ya_sc[...], wa_ref[...].astype(BF16), preferred_element_type=F32)
    acc += jnp.dot(yn_sc[...], wn_ref[...].astype(BF16), preferred_element_type=F32)
    acc += jnp.dot(yc_sc[...], wc_ref[...].astype(BF16), preferred_element_type=F32)
    o_ref[...] = acc


def out_proj(oa, on, oc, mix_g, w_out, layer, *, tm=1024, tn=512):
    M = oa.shape[0]
    C = oc.shape[1]
    D = w_out.shape[-1]
    tm = _pick(M, tm)
    tn = _pick(D, tn)
    assert A_W == N_W and (A_W + N_W) % C == 0
    ga = mix_g[layer:layer + 1, :A_W]
    gn = mix_g[layer:layer + 1, A_W:A_W + N_W]
    gc = mix_g[layer:layer + 1, A_W + N_W:]
    return pl.pallas_call(
        _out_proj_kernel,
        out_shape=jax.ShapeDtypeStruct((M, D), F32),
        grid=(M // tm, D // tn),
        in_specs=[pl.BlockSpec((tm, A_W), lambda i, j: (i, 0), pipeline_mode=pl.Buffered(1)),
                  pl.BlockSpec((tm, N_W), lambda i, j: (i, 0), pipeline_mode=pl.Buffered(1)),
                  pl.BlockSpec((tm, C), lambda i, j: (i, 0), pipeline_mode=pl.Buffered(1)),
                  pl.BlockSpec((1, A_W), lambda i, j: (0, 0)),
                  pl.BlockSpec((1, N_W), lambda i, j: (0, 0)),
                  pl.BlockSpec((1, C), lambda i, j: (0, 0)),
                  pl.BlockSpec((None, A_W, tn), lambda i, j: (layer, 0, j)),
                  pl.BlockSpec((None, N_W, tn), lambda i, j: (layer, 1, j)),
                  pl.BlockSpec((None, C, tn), lambda i, j: (layer, (A_W + N_W) // C, j))],
        out_specs=pl.BlockSpec((tm, tn), lambda i, j: (i, j)),
        scratch_shapes=[pltpu.VMEM((tm, A_W), BF16), pltpu.VMEM((tm, N_W), BF16), pltpu.VMEM((tm, C), BF16)],
        compiler_params=_cparams(("parallel", "arbitrary")),
        name="out_proj",
    )(oa, on, oc, ga, gn, gc, w_out, w_out, w_out)


def _conv_kernel(cur_ref, prev_ref, w_ref, o_ref, *, C):
    i = pl.program_id(1)
    cur = cur_ref[0]
    u, bg, cg = (cur[:, GATE_W + k * C:GATE_W + (k + 1) * C] for k in range(3))
    z = cg * u
    prev = prev_ref[0]
    zp = prev[:, GATE_W + 2 * C:GATE_W + 3 * C] * prev[:, GATE_W:GATE_W + C]
    zp = jnp.where(i > 0, zp, 0.0)
    rows = lax.broadcasted_iota(jnp.int32, z.shape, 0)
    z1 = jnp.where(rows >= 1, pltpu.roll(z, 1, axis=0), zp[7:8, :])
    z2 = jnp.where(rows >= 2, pltpu.roll(z, 2, axis=0), jnp.where(rows == 1, zp[7:8, :], zp[6:7, :]))
    w = w_ref[...]
    o_ref[0] = bg * (w[0:1, :] * z2 + w[1:2, :] * z1 + w[2:3, :] * z)


def short_conv(pb, conv_w_l, *, tm=512):
    B, S, W = pb.shape
    C = conv_w_l.shape[-1]
    tm = _pick(S, tm)
    return pl.pallas_call(
        functools.partial(_conv_kernel, C=C),
        out_shape=jax.ShapeDtypeStruct((B, S, C), F32),
        grid=(B, S // tm),
        in_specs=[pl.BlockSpec((1, tm, W), lambda b, i: (b, i, 0)),
                  pl.BlockSpec((1, 8, W), lambda b, i: (b, jnp.maximum(i * (tm // 8) - 1, 0), 0)),
                  pl.BlockSpec((CONV_W, C), lambda b, i: (0, 0))],
        out_specs=pl.BlockSpec((1, tm, C), lambda b, i: (b, i, 0)),
        compiler_params=_cparams(("parallel", "parallel")),
        name="short_conv",
    )(pb, pb, conv_w_l)


def _dilated_kernel(sl_ref, q_ref, k_ref, v_ref, o_ref, qf, kf, vf, u_sc, m_sc, l_sc):
    h = pl.program_id(1)
    sb = pl.program_id(2)
    base = sb * A_SUPER

    @pl.when(sb == 0)
    def _():
        kf[...] = k_ref[0].astype(F32)
        vf[...] = v_ref[0].astype(F32)

    qf[...] = q_ref[0].astype(F32)
    slope = sl_ref[h]
    qi = lax.broadcasted_iota(jnp.int32, (A_BLK, A_BLK), 0)
    kj = lax.broadcasted_iota(jnp.int32, (A_BLK, A_BLK), 1)
    diff = (qi - kj).astype(F32)
    cur_ok = kj <= qi
    prev_band = kj >= qi
    nt = (((1,), (1,)), ((), ()))

    for br, (_, d) in enumerate(A_BRANCHES):
        span = A_BLK * d
        sd = slope * float(d)
        bias_c = sd * diff
        bias_p = sd * (diff + float(A_BLK))

        def ds(start):
            return pl.ds(start, A_BLK, stride=d) if d > 1 else pl.ds(start, A_BLK)

        for grp in range(A_SUPER // A_BLK // A_GROUP):
            blocks = []
            for i in range(grp * A_GROUP, (grp + 1) * A_GROUP):
                loc = (i // d) * span + (i % d)
                has_prev = (sb > 0) if loc < span else True
                start = base + loc
                pstart = jnp.where(has_prev, start - span, start) if loc < span else start - span
                blocks.append((ds(loc), ds(start), ds(pstart), has_prev))
            s_c, s_p, hp = [], [], []
            for lrow, rows, prow, has_prev in blocks:
                q = qf[lrow, :].astype(BF16)
                s_c.append(lax.dot_general(q, kf[rows, :].astype(BF16), nt, preferred_element_type=F32))
                s_p.append(lax.dot_general(q, kf[prow, :].astype(BF16), nt, preferred_element_type=F32))
                hp.append(jnp.full((1, 1), jnp.where(has_prev, 0.0, NEG), F32))
            s_c = jnp.where(cur_ok[None], jnp.stack(s_c) - bias_c[None], NEG)
            s_p = jnp.where(prev_band[None], jnp.stack(s_p) - bias_p[None] + jnp.stack(hp), NEG)
            m = jnp.max(jnp.maximum(s_c, s_p), axis=-1, keepdims=True)
            e_c = jnp.exp(s_c - m)
            e_p = jnp.exp(s_p - m)
            l = jnp.sum(e_c + e_p, axis=-1, keepdims=True)
            for n, (lrow, rows, prow, _) in enumerate(blocks):
                u = jnp.dot(e_c[n].astype(BF16), vf[rows, :].astype(BF16), preferred_element_type=F32)
                u += jnp.dot(e_p[n].astype(BF16), vf[prow, :].astype(BF16), preferred_element_type=F32)
                u_sc[br, lrow, :] = u
                m_sc[br, lrow, :] = jnp.broadcast_to(m[n], (A_BLK, HEAD_DIM))
                l_sc[br, lrow, :] = jnp.broadcast_to(l[n], (A_BLK, HEAD_DIM))

    nb = len(A_BRANCHES)
    mt = m_sc[0]
    for br in range(1, nb):
        mt = jnp.maximum(mt, m_sc[br])
    num = jnp.zeros((A_SUPER, HEAD_DIM), F32)
    den = jnp.zeros((A_SUPER, HEAD_DIM), F32)
    for br in range(nb):
        w = jnp.exp(m_sc[br] - mt)
        num += w * u_sc[br]
        den += w * l_sc[br]
    o_ref[0] = num / den


def dilated_attention(pa, slopes):
    B, S, _ = pa.shape
    assert S % A_SUPER == 0
    grid_spec = pltpu.PrefetchScalarGridSpec(
        num_scalar_prefetch=1,
        grid=(B, A_HEADS, S // A_SUPER),
        in_specs=[pl.BlockSpec((1, A_SUPER, HEAD_DIM), lambda b, h, s, sl: (b, s, h)),
                  pl.BlockSpec((1, S, HEAD_DIM), lambda b, h, s, sl: (b, 0, A_HEADS + h)),
                  pl.BlockSpec((1, S, HEAD_DIM), lambda b, h, s, sl: (b, 0, 2 * A_HEADS + h))],
        out_specs=pl.BlockSpec((1, A_SUPER, HEAD_DIM), lambda b, h, s, sl: (b, s, h)),
        scratch_shapes=[pltpu.VMEM((A_SUPER, HEAD_DIM), F32),
                        pltpu.VMEM((S, HEAD_DIM), F32),
                        pltpu.VMEM((S, HEAD_DIM), F32),
                        pltpu.VMEM((len(A_BRANCHES), A_SUPER, HEAD_DIM), F32),
                        pltpu.VMEM((len(A_BRANCHES), A_SUPER, HEAD_DIM), F32),
                        pltpu.VMEM((len(A_BRANCHES), A_SUPER, HEAD_DIM), F32)],
    )
    return pl.pallas_call(
        _dilated_kernel,
        out_shape=jax.ShapeDtypeStruct((B, S, A_W), F32),
        grid_spec=grid_spec,
        compiler_params=_cparams(("parallel", "parallel", "arbitrary")),
        name="dilated_attention",
    )(slopes, pa, pa, pa)


def _compress_kernel(kc_ref, vc_ref, wk_ref, wv_ref, pe_ref, ko_ref, vo_ref, xf, bsc, *, nc):
    for x_ref, w_ref, o_ref in ((kc_ref, wk_ref, ko_ref), (vc_ref, wv_ref, vo_ref)):
        xf[...] = x_ref[0].astype(F32)
        first = jnp.zeros((nc, HEAD_DIM), F32)
        second = jnp.zeros((nc, HEAD_DIM), F32)
        for j in range(CMP_STRIDE):
            xj = xf[pl.ds(j, nc, stride=CMP_STRIDE), :]
            wa = w_ref[j * HEAD_DIM:(j + 1) * HEAD_DIM, :].astype(BF16)
            wb = w_ref[(CMP_STRIDE + j) * HEAD_DIM:(CMP_STRIDE + j + 1) * HEAD_DIM, :].astype(BF16)
            first += jnp.dot((xj + pe_ref[j:j + 1, :]).astype(BF16), wa, preferred_element_type=F32)
            second += jnp.dot((xj + pe_ref[CMP_STRIDE + j:CMP_STRIDE + j + 1, :]).astype(BF16), wb,
                              preferred_element_type=F32)
        bsc[0:nc, :] = second
        bsc[nc:nc + 8, :] = jnp.zeros((8, HEAD_DIM), F32)
        out = first + bsc[1:nc + 1, :]
        row = lax.broadcasted_iota(jnp.int32, out.shape, 0)
        o_ref[0, 0] = jnp.where(row < nc - 1, out, 0.0).astype(o_ref.dtype)


def nsa_compress(pa, cmp_wk, cmp_wv, cmp_pe, layer):
    B, S, _ = pa.shape
    nc = S // CMP_STRIDE
    kc0 = (3 * A_W + N_W) // HEAD_DIM
    vc0 = kc0 + NSA_KV_HEADS
    wspec = pl.BlockSpec((None, CMP_LEN * HEAD_DIM, HEAD_DIM), lambda b, g: (layer, 0, 0))
    ospec = pl.BlockSpec((1, 1, nc, HEAD_DIM), lambda b, g: (b, g, 0, 0))
    oshape = jax.ShapeDtypeStruct((B, NSA_KV_HEADS, nc, HEAD_DIM), BF16)
    return pl.pallas_call(
        functools.partial(_compress_kernel, nc=nc),
        out_shape=[oshape, oshape],
        grid=(B, NSA_KV_HEADS),
        in_specs=[pl.BlockSpec((1, S, HEAD_DIM), lambda b, g: (b, 0, kc0 + g)),
                  pl.BlockSpec((1, S, HEAD_DIM), lambda b, g: (b, 0, vc0 + g)),
                  wspec, wspec,
                  pl.BlockSpec((None, CMP_LEN, HEAD_DIM), lambda b, g: (layer, 0, 0))],
        out_specs=[ospec, ospec],
        scratch_shapes=[pltpu.VMEM((S, HEAD_DIM), F32), pltpu.VMEM((nc + 8, HEAD_DIM), F32)],
        compiler_params=_cparams(("parallel", "parallel")),
        name="nsa_compress",
    )(pa, pa, cmp_wk, cmp_wv, cmp_pe)


def _gate_cols(gt_ref, g, M):
    sig = 1.0 / (1.0 + jnp.exp(-gt_ref[0]))
    lane = lax.broadcasted_iota(jnp.int32, sig.shape, 1)
    return [[jnp.sum(jnp.where(lane == (g * M + m) * 3 + b, sig, 0.0), axis=-1, keepdims=True)
             for b in range(3)] for m in range(M)]


def _nsa_select_kernel(sl_ref, q_ref, gt_ref, kc_ref, vc_ref, op_ref, mn_ref, *, TQ, nc):
    g = pl.program_id(1)
    q0 = pl.program_id(2) * TQ
    M = NSA_GROUP
    nt = (((1,), (1,)), ((), ()))

    q = q_ref[0]
    qs = jnp.concatenate([q[:, m * HEAD_DIM:(m + 1) * HEAD_DIM] for m in range(M)], axis=0)
    slope_col = jnp.concatenate([jnp.full((TQ, 1), sl_ref[g * M + m], F32) for m in range(M)], axis=0)
    tq_i = q0 + lax.broadcasted_iota(jnp.int32, (TQ, 1), 0)
    t_col = jnp.concatenate([tq_i] * M, axis=0).astype(F32)

    def softmax_rows(s, ok):
        s = jnp.where(ok, s, NEG)
        m = jnp.max(s, axis=-1, keepdims=True)
        e = jnp.where(ok, jnp.exp(s - m), 0.0)
        den = jnp.sum(e, axis=-1, keepdims=True)
        return e / jnp.maximum(den, TINY)

    cidx = lax.broadcasted_iota(jnp.int32, (1, nc), 1)
    c_end = (cidx * CMP_STRIDE + (CMP_LEN - 1)).astype(F32)
    dist = t_col - c_end
    s = lax.dot_general(qs, kc_ref[0, 0], nt, preferred_element_type=F32) - slope_col * dist
    p_cmp = softmax_rows(s, (dist >= 0) & (cidx < nc - 1))
    o_cmp = jnp.dot(p_cmp.astype(BF16), vc_ref[0, 0], preferred_element_type=F32)

    psum = p_cmp[0:TQ]
    for m in range(1, M):
        psum = psum + p_cmp[m * TQ:(m + 1) * TQ]
    ci = lax.broadcasted_iota(jnp.int32, (nc, LANES), 0)
    ni = lax.broadcasted_iota(jnp.int32, (nc, LANES), 1)
    overlap = ((ci * CMP_STRIDE < ni * SEL_BLK + SEL_BLK) & (ci * CMP_STRIDE + CMP_LEN - 1 >= ni * SEL_BLK)
               & (ci < nc - 1))
    overlap = jnp.where(overlap, 1.0, 0.0).astype(BF16)
    imp = jnp.zeros((TQ, LANES), F32)
    for piece in _split3(psum):
        imp = imp + jnp.dot(piece, overlap, preferred_element_type=F32)
    jb = lax.broadcasted_iota(jnp.int32, (TQ, LANES), 1)
    cur = tq_i // SEL_BLK
    forced = (jb == 0) | (jb == cur) | (jb == cur - 1)
    imp = jnp.where(forced, imp + FORCE, imp)
    causal_blk = jb * SEL_BLK <= tq_i
    imp = jnp.where(causal_blk, imp, NEG)

    work = imp
    memb = jnp.zeros((TQ, LANES), F32)
    jbf = jb.astype(F32)
    for _ in range(TOP_N):
        mx = jnp.max(work, axis=-1, keepdims=True)
        first = jnp.min(jnp.where(work == mx, jbf, float(LANES)), axis=-1, keepdims=True)
        pick = jbf == first
        memb = jnp.where(pick, 1.0, memb)
        work = jnp.where(pick, LOWEST, work)
    chosen = (memb > 0.5) & causal_blk
    mn_ref[0, 0] = jnp.where(chosen, 0.0, NEG).astype(mn_ref.dtype)
    gates = _gate_cols(gt_ref, g, M)
    for m in range(M):
        op_ref[0, :, m * HEAD_DIM:(m + 1) * HEAD_DIM] = gates[m][0] * o_cmp[m * TQ:(m + 1) * TQ]


def nsa_select(pa, pb, k_cmp, v_cmp, slopes, *, TQ=512):
    B, S, _ = pa.shape
    nc = k_cmp.shape[2]
    M = NSA_GROUP
    TQ = _pick(S, TQ)
    assert S // SEL_BLK <= LANES
    q0 = 3 * A_W // (M * HEAD_DIM)
    cmp_spec = pl.BlockSpec((1, 1, nc, HEAD_DIM), lambda b, g, t, sl: (b, g, 0, 0))
    grid_spec = pltpu.PrefetchScalarGridSpec(
        num_scalar_prefetch=1,
        grid=(B, NSA_KV_HEADS, S // TQ),
        in_specs=[pl.BlockSpec((1, TQ, M * HEAD_DIM), lambda b, g, t, sl: (b, t, q0 + g)),
                  pl.BlockSpec((1, TQ, LANES), lambda b, g, t, sl: (b, t, 0)),
                  cmp_spec, cmp_spec],
        out_specs=[pl.BlockSpec((1, TQ, M * HEAD_DIM), lambda b, g, t, sl: (b, t, g)),
                   pl.BlockSpec((1, 1, TQ, LANES), lambda b, g, t, sl: (b, g, t, 0))],
    )
    return pl.pallas_call(
        functools.partial(_nsa_select_kernel, TQ=TQ, nc=nc),
        out_shape=[jax.ShapeDtypeStruct((B, S, N_W), F32),
                   jax.ShapeDtypeStruct((B, NSA_KV_HEADS, S, LANES), BF16)],
        grid_spec=grid_spec,
        compiler_params=_cparams(("parallel", "parallel", "parallel")),
        name="nsa_select",
    )(slopes, pa, pb, k_cmp, v_cmp)


def _nsa_sweep_kernel(sl_ref, q_ref, gt_ref, op_ref, mn_ref, ks_ref, vs_ref, kw_ref, vw_ref, o_ref,
                      qa_sc, mx_sc, l_sc, acc_sc, flag_ref, act_ref, *, TQ, n_kt, G):
    g = pl.program_id(1)
    qt = pl.program_id(2)
    M = NSA_GROUP
    R = M * TQ
    kt = TQ
    bpk = kt // SEL_BLK
    nt = (((1,), (1,)), ((), ()))

    q = q_ref[0]
    mn = mn_ref[0, 0]
    for m in range(M):
        qa_sc[m * TQ:(m + 1) * TQ, 0:HEAD_DIM] = q[:, m * HEAD_DIM:(m + 1) * HEAD_DIM]
        qa_sc[m * TQ:(m + 1) * TQ, HEAD_DIM:2 * HEAD_DIM] = mn
    slope_rep = jnp.concatenate([jnp.full((TQ, kt), sl_ref[g * M + m], F32) for m in range(M)], axis=0)
    trel = jnp.concatenate([lax.broadcasted_iota(jnp.int32, (TQ, 1), 0)] * M, axis=0)
    lane = lax.broadcasted_iota(jnp.int32, (1, kt), 1)

    cnt = jnp.sum(jnp.where(mn.astype(F32) == 0.0, 1.0, 0.0), axis=0, keepdims=True)
    cnt = jnp.broadcast_to(cnt, (8, LANES))
    per_tile = cnt
    for b in range(1, bpk):
        per_tile = per_tile + pltpu.roll(cnt, LANES - b, axis=1)
    for j in range(n_kt):
        flag_ref[j] = (per_tile[0, j * bpk] > 0.0).astype(jnp.int32)

    act_ref[0] = 0

    def compact(j, n):
        act_ref[n] = j
        return n + flag_ref[j]

    n_act = lax.fori_loop(0, qt, compact, 0)
    for u in range(1, G):
        act_ref[n_act + u - 1] = act_ref[0]
    n_grp = (n_act + G - 1) // G

    def rows_of(j):
        return pl.ds(pl.multiple_of(j * kt, kt), kt)

    def sel_scores(j, shift):
        ki = lax.broadcasted_iota(jnp.int32, (kt, LANES), 0)
        bi = lax.broadcasted_iota(jnp.int32, (kt, LANES), 1)
        onehot = jnp.where(bi == j * bpk + ki // SEL_BLK, 1.0, 0.0).astype(BF16)
        s = lax.dot_general(qa_sc[...], jnp.concatenate([ks_ref[0, rows_of(j), :], onehot], axis=1), nt,
                            preferred_element_type=F32)
        return s + slope_rep * (((j - qt) * kt + lane).astype(F32) + shift)

    mx_sc[...] = jnp.full((R, kt), 0.5 * NEG, F32)

    def max_body(gi, carry):
        m = mx_sc[...]
        for u in range(G):
            m = jnp.maximum(m, sel_scores(act_ref[gi * G + u], 0.0))
        mx_sc[...] = m
        return carry

    lax.fori_loop(0, n_grp, max_body, 0)
    s_diag = jnp.where(lane <= trel, sel_scores(qt, 0.0), NEG)
    m_row = jnp.max(jnp.maximum(mx_sc[...], s_diag), axis=-1, keepdims=True)
    mx_sc[...] = jnp.broadcast_to(m_row, (R, kt))

    p = jnp.exp(s_diag - m_row)
    l_sc[...] = p
    acc_sc[...] = jnp.dot(p.astype(BF16), vs_ref[0, rows_of(qt), :], preferred_element_type=F32)

    def acc_body(gi, carry):
        m = mx_sc[...]
        l = l_sc[...]
        ps, vs = [], []
        for u in range(G):
            idx = gi * G + u
            j = act_ref[idx]
            p = jnp.exp(sel_scores(j, jnp.where(idx < n_act, 0.0, -1e34)) - m)
            l = l + p
            ps.append(p.astype(BF16))
            vs.append(vs_ref[0, rows_of(j), :])
        l_sc[...] = l
        acc_sc[...] += jnp.dot(jnp.concatenate(ps, axis=1), jnp.concatenate(vs, axis=0), preferred_element_type=F32)
        return carry

    lax.fori_loop(0, n_grp, acc_body, 0)
    o_sel = acc_sc[...] / jnp.maximum(jnp.sum(l_sc[...], axis=-1, keepdims=True), TINY)

    n_w = NSA_WIN // kt + 1
    s_w, v_w = [], []
    for u in range(n_w):
        jc = jnp.maximum(qt - (n_w - 1) + u, 0)
        prel = (u - (n_w - 1)) * kt + lane
        d = trel - prel
        ok = (d >= 0) & (d < NSA_WIN) & (prel >= -qt * kt)
        s = lax.dot_general(qa_sc[:, 0:HEAD_DIM], kw_ref[0, rows_of(jc), :], nt, preferred_element_type=F32)
        s_w.append(jnp.where(ok, s + slope_rep * prel.astype(F32), NEG))
        v_w.append(vw_ref[0, rows_of(jc), :])
    m_el = s_w[0]
    for s in s_w[1:]:
        m_el = jnp.maximum(m_el, s)
    m_row = jnp.max(m_el, axis=-1, keepdims=True)
    p_w = [jnp.exp(s - m_row) for s in s_w]
    l_el = p_w[0]
    for p in p_w[1:]:
        l_el = l_el + p
    o_win = jnp.dot(jnp.concatenate([p.astype(BF16) for p in p_w], axis=1), jnp.concatenate(v_w, axis=0),
                    preferred_element_type=F32) / jnp.maximum(jnp.sum(l_el, axis=-1, keepdims=True), TINY)

    gates = _gate_cols(gt_ref, g, M)
    for m in range(M):
        rs = slice(m * TQ, (m + 1) * TQ)
        cs = slice(m * HEAD_DIM, (m + 1) * HEAD_DIM)
        o_ref[0, :, cs] = op_ref[0, :, cs] + gates[m][1] * o_sel[rs] + gates[m][2] * o_win[rs]


def nsa_sweep(pa, pb, o_part, mneg, slopes, *, TQ=128, group=4):
    B, S, _ = pa.shape
    M = NSA_GROUP
    assert TQ == LANES and TQ % SEL_BLK == 0 and NSA_WIN % TQ == 0 and S % TQ == 0
    q0 = 3 * A_W // (M * HEAD_DIM)
    kv0 = (3 * A_W + N_W) // HEAD_DIM + 2 * NSA_KV_HEADS
    kv = lambda n: pl.BlockSpec((1, S, HEAD_DIM), lambda b, g, t, sl: (b, 0, kv0 + n * NSA_KV_HEADS + g))
    qspec = pl.BlockSpec((1, TQ, M * HEAD_DIM), lambda b, g, t, sl: (b, t, q0 + g))
    ospec = pl.BlockSpec((1, TQ, M * HEAD_DIM), lambda b, g, t, sl: (b, t, g))
    grid_spec = pltpu.PrefetchScalarGridSpec(
        num_scalar_prefetch=1,
        grid=(B, NSA_KV_HEADS, S // TQ),
        in_specs=[qspec,
                  pl.BlockSpec((1, TQ, LANES), lambda b, g, t, sl: (b, t, 0)),
                  ospec,
                  pl.BlockSpec((1, 1, TQ, LANES), lambda b, g, t, sl: (b, g, t, 0)),
                  kv(0), kv(1), kv(2), kv(3)],
        out_specs=ospec,
        scratch_shapes=[pltpu.VMEM((M * TQ, 2 * HEAD_DIM), BF16),
                        pltpu.VMEM((M * TQ, TQ), F32), pltpu.VMEM((M * TQ, TQ), F32),
                        pltpu.VMEM((M * TQ, HEAD_DIM), F32),
                        pltpu.SMEM((S // TQ,), jnp.int32),
                        pltpu.SMEM((S // TQ + group,), jnp.int32)],
    )
    return pl.pallas_call(
        functools.partial(_nsa_sweep_kernel, TQ=TQ, n_kt=S // TQ, G=group),
        out_shape=jax.ShapeDtypeStruct((B, S, N_W), F32),
        grid_spec=grid_spec,
        compiler_params=_cparams(("parallel", "parallel", "arbitrary")),
        name="nsa_sweep",
    )(slopes, pa, pb, o_part, mneg, pa, pa, pa, pa)


def kernel(x, c, ada_w, ada_b, w_in, conv_w, cmp_wk, cmp_wv, cmp_pe, mix_g, w_out, ln1_g, ln1_b, ln2_g, ln2_b,
           ffn_w_gate, ffn_w_up, ffn_w_down, moe_router, moe_router_b, moe_w_gate, moe_w_up, moe_w_down):
    B, S, D = x.shape
    depth = ada_w.shape[0]
    C = conv_w.shape[-1]
    N = B * S
    alpha = (2 * depth) ** 0.25
    assert D == A_W + N_W + C and w_in.shape[-1] == ATT_W + GATE_W + 3 * C and ATT_W % PB_TN == 0
    pb_w = pl.cdiv(GATE_W + 3 * C, PB_TN) * PB_TN
    w_in_b = repack_weight(jnp.transpose(w_in, (2, 0, 1)), ATT_W + pb_w, tn=PB_TN)

    n_heads = A_HEADS + NSA_HEADS
    sl = jnp.exp2(-8.0 * jnp.arange(1, n_heads + 1, dtype=F32) / n_heads)
    sl_a, sl_n = sl[0::2], sl[1::2]
    scale = HEAD_DIM ** -0.5
    col = jnp.arange(ATT_W)
    is_q = (col < A_W) | ((col >= 3 * A_W) & (col < 3 * A_W + N_W))
    q_scale = jnp.where(is_q, scale, 1.0).astype(F32)[None, :]

    cond = jnp.zeros((8, D), F32).at[:B].set(c * jax.nn.sigmoid(c)).astype(BF16)
    mods = [matmul(cond, ada_w, i, 6 * D, tm=8, tn=1024)[:B] + ada_b[i] for i in range(depth)]
    h = None
    gate = None
    for i in range(depth):
        sh1, sc1, g1, sh2, sc2, g2 = [m[:, None, :] for m in jnp.split(mods[i], 6, axis=-1)]
        if i == 0:
            h = ln_mod(x, sc1, sh1)

        hf = h.reshape(N, D)
        pa = matmul(hf, w_in_b, i, ATT_W, scale=q_scale, out_dtype=BF16, tm=2048, tn=256).reshape(B, S, ATT_W)
        pb = matmul(hf, w_in_b, i, pb_w, col0=ATT_W, tm=2048, tn=PB_TN).reshape(B, S, pb_w)
        oa = dilated_attention(pa, sl_a)
        k_cmp, v_cmp = nsa_compress(pa, cmp_wk, cmp_wv, cmp_pe, i)
        o_part, mneg = nsa_select(pa, pb, k_cmp, v_cmp, sl_n)
        on = nsa_sweep(pa, pb, o_part, mneg, sl_n)
        oc = short_conv(pb, conv_w[i])
        y = out_proj(oa.reshape(N, A_W), on.reshape(N, N_W), oc.reshape(N, C), mix_g, w_out, i).reshape(B, S, D)

        moe_layer = i % 2 == 1
        j = i // 2
        if moe_layer:
            rw = jnp.zeros((D, LANES), F32).at[:, :N_EXPERTS].set(moe_router[j])
            rb = jnp.zeros((1, LANES), F32).at[0, :N_EXPERTS].set(moe_router_b[j])
            x, h, table = res_ln(x, y, g1, ln1_g[i:i + 1], ln1_b[i:i + 1], alpha, sc=sc2, sh=sh2,
                                 router_w=rw, router_b=rb)
        else:
            x, h = res_ln(x, y, g1, ln1_g[i:i + 1], ln1_b[i:i + 1], alpha, sc=sc2, sh=sh2)

        hf = h.reshape(N, D)
        if moe_layer:
            TM = _pick(N, 1024)
            plan = moe_plan(table.reshape(N, LANES), TM)
            a = moe_gate_up(hf, moe_w_gate, moe_w_up, j, plan, TM=TM, chunk=_pick(TM, 512))
            ye = moe_down(a, moe_w_down, j, plan, TM=TM)
            y = moe_combine(ye, plan, N)
        else:
            a = gate_up(hf, ffn_w_gate, ffn_w_up, (j,))
            y = matmul(a, ffn_w_down, j, D, tm=2048, tn=1024, tk=1024)
        y = y.reshape(B, S, D)

        if i + 1 < depth:
            sh_n, sc_n = mods[i + 1][:, None, :D], mods[i + 1][:, None, D:2 * D]
            x, h = res_ln(x, y, g2, ln2_g[i:i + 1], ln2_b[i:i + 1], alpha, sc=sc_n, sh=sh_n)
        else:
            (x,) = res_ln(x, y, g2, ln2_g[i:i + 1], ln2_b[i:i + 1], alpha)
    return x
```

```python
import functools

import jax
import jax.numpy as jnp
from jax import lax
from jax.experimental import pallas as pl
from jax.experimental.pallas import tpu as pltpu

F32 = jnp.float32
BF16 = jnp.bfloat16

HEAD_DIM = 128
A_HEADS = 12
A_BRANCHES = ((128, 1), (512, 4), (2048, 16))
A_BLK = 128
A_SUPER = A_BLK * max(d for _, d in A_BRANCHES)
A_GROUP = 8
NSA_HEADS = 12
NSA_KV_HEADS = 3
NSA_GROUP = NSA_HEADS // NSA_KV_HEADS
CMP_LEN = 32
CMP_STRIDE = 16
SEL_BLK = 64
TOP_N = 16
NSA_WIN = 512
CONV_W = 3
N_EXPERTS = 8
LN_EPS = 1e-5
NEG = -1e30
TINY = 1e-30
FORCE = 1e4
LOWEST = -3e38

A_W = A_HEADS * HEAD_DIM
N_W = NSA_HEADS * HEAD_DIM
KV_W = NSA_KV_HEADS * HEAD_DIM
ATT_W = 3 * A_W + N_W + 6 * KV_W
GATE_W = NSA_HEADS * 3
LANES = 128
PB_TN = 2 * LANES
VMEM_LIMIT = 56 * 1024 * 1024


def _cparams(sem):
    return pltpu.CompilerParams(dimension_semantics=sem, vmem_limit_bytes=VMEM_LIMIT)


def _pick(n, pref):
    if n <= pref:
        return n
    t = pref
    while n % t:
        t //= 2
    return t


def _mm_kernel(*refs, nk, k_tail, has_scale):
    if has_scale:
        x_ref, w_ref, s_ref, o_ref, acc_ref = refs
    else:
        x_ref, w_ref, o_ref, acc_ref = refs
        s_ref = None

    def product(tail):
        x = x_ref[...]
        w = w_ref[...]
        if tail:
            x = jnp.where(lax.broadcasted_iota(jnp.int32, x.shape, 1) < tail, x, jnp.zeros_like(x))
            w = jnp.where(lax.broadcasted_iota(jnp.int32, w.shape, 0) < tail, w, jnp.zeros_like(w))
        return jnp.dot(x, w.astype(BF16), preferred_element_type=F32)

    def finish(acc):
        if s_ref is not None:
            acc = acc * s_ref[...]
        o_ref[...] = acc.astype(o_ref.dtype)

    if nk == 1:
        finish(product(k_tail))
        return
    k = pl.program_id(2)

    @pl.when(k == 0)
    def _():
        acc_ref[...] = product(0)

    @pl.when((k > 0) & (k < nk - 1))
    def _():
        acc_ref[...] += product(0)

    @pl.when(k == nk - 1)
    def _():
        finish(acc_ref[...] + product(k_tail))


def matmul(x, w, w_idx, n_cols, *, col0=0, scale=None, out_dtype=F32, tm=1024, tn=512, tk=None):
    M, K = x.shape
    tm = _pick(M, tm)
    tn = _pick(n_cols, tn)
    tk = K if tk is None or tk >= K else tk
    assert col0 % tn == 0 and col0 + n_cols <= w.shape[-1]
    nk = pl.cdiv(K, tk)
    cb = col0 // tn
    lead = (w_idx,) if w.ndim == 3 else ()
    wblk = ((None,) if w.ndim == 3 else ()) + (tk, tn)
    in_specs = [pl.BlockSpec((tm, tk), lambda i, j, k: (i, k)),
                pl.BlockSpec(wblk, lambda i, j, k: lead + (k, cb + j))]
    args = [x, w]
    if scale is not None:
        in_specs.append(pl.BlockSpec((1, tn), lambda i, j, k: (0, j)))
        args.append(scale)
    return pl.pallas_call(
        functools.partial(_mm_kernel, nk=nk, k_tail=K % tk, has_scale=scale is not None),
        out_shape=jax.ShapeDtypeStruct((M, n_cols), out_dtype),
        grid=(M // tm, n_cols // tn, nk),
        in_specs=in_specs,
        out_specs=pl.BlockSpec((tm, tn), lambda i, j, k: (i, j)),
        scratch_shapes=[pltpu.VMEM((tm, tn), F32)],
        compiler_params=_cparams(("parallel", "parallel", "arbitrary")),
        name="matmul",
    )(*args)


def _repack_kernel(wt_ref, o_ref, *, n_valid):
    tn = wt_ref.shape[0]
    col = pl.program_id(0) * tn + lax.broadcasted_iota(jnp.int32, (1, tn), 1)
    for layer in range(wt_ref.shape[1]):
        w = jnp.transpose(wt_ref[:, layer, :]).astype(BF16)
        o_ref[layer] = jnp.where(col < n_valid, w, jnp.zeros_like(w))


def repack_weight(wt, n_cols, *, tn=256):
    N, L, K = wt.shape
    assert n_cols % tn == 0 and n_cols >= N
    return pl.pallas_call(
        functools.partial(_repack_kernel, n_valid=N),
        out_shape=jax.ShapeDtypeStruct((L, K, n_cols), BF16),
        grid=(n_cols // tn,),
        in_specs=[pl.BlockSpec((tn, L, K), lambda j: (j, 0, 0))],
        out_specs=pl.BlockSpec((L, K, tn), lambda j: (0, 0, j)),
        compiler_params=_cparams(("parallel",)),
        name="repack_weight",
    )(wt)


def _gate_up_kernel(x_ref, wg_ref, wu_ref, o_ref):
    x = x_ref[...]
    g = jnp.dot(x, wg_ref[...].astype(BF16), preferred_element_type=F32)
    u = jnp.dot(x, wu_ref[...].astype(BF16), preferred_element_type=F32)
    o_ref[...] = (g * (1.0 / (1.0 + jnp.exp(-g))) * u).astype(o_ref.dtype)


def gate_up(x, wg, wu, lead, *, tm=1024, tn=256):
    M, K = x.shape
    F = wg.shape[-1]
    tm = _pick(M, tm)
    tn = _pick(F, tn)
    nlead = len(lead)
    wspec = pl.BlockSpec((None,) * nlead + (K, tn), lambda i, j: lead + (0, j))
    return pl.pallas_call(
        _gate_up_kernel,
        out_shape=jax.ShapeDtypeStruct((M, F), BF16),
        grid=(M // tm, F // tn),
        in_specs=[pl.BlockSpec((tm, K), lambda i, j: (i, 0)), wspec, wspec],
        out_specs=pl.BlockSpec((tm, tn), lambda i, j: (i, j)),
        compiler_params=_cparams(("parallel", "parallel")),
        name="gate_up",
    )(x, wg, wu)


def moe_plan(table, tile_rows):
    N = table.shape[0]
    e = jnp.concatenate([table[:, 0], table[:, 1]]).astype(jnp.int32)
    w = jnp.concatenate([table[:, 2], table[:, 3]])
    onehot = (e[:, None] == jnp.arange(N_EXPERTS)[None, :]).astype(jnp.int32)
    csum = jnp.cumsum(onehot, axis=0)
    rank = jnp.sum(csum * onehot, axis=1) - 1
    tiles = (csum[-1] + tile_rows - 1) // tile_rows
    tile_end = jnp.cumsum(tiles)
    row = (tile_end - tiles)[e] * tile_rows + rank
    T = 2 * N // tile_rows + N_EXPERTS
    tok = (jnp.arange(2 * N, dtype=jnp.int32) % N).astype(F32)
    both = jnp.zeros((T * tile_rows, 2), F32).at[row].set(jnp.stack([tok, w], axis=1))
    src = both[:, 0].astype(jnp.int32)
    row_gate = both[:, 1]
    n_used = tile_end[-1:].astype(jnp.int32)
    t_idx = jnp.minimum(jnp.arange(T), n_used[0] - 1)
    tile_expert = jnp.minimum(jnp.searchsorted(tile_end, t_idx, side="right"), N_EXPERTS - 1).astype(jnp.int32)
    return dict(src=src.reshape(T, 1, tile_rows), row_gate=row_gate.reshape(T * tile_rows, 1),
                tile_expert=tile_expert, n_used=n_used, rows=jnp.stack([row[:N], row[N:]], axis=0), T=T)


def _row_copy(src_hbm, row, dst, slot, sem):
    return pltpu.make_async_copy(src_hbm.at[pl.ds(row, 1), :], dst.at[pl.ds(slot, 1), :], sem)


def _moe_gate_up_kernel(te_ref, nu_ref, src_ref, nxt_ref, h_hbm, wg_ref, wu_ref, o_ref, xf, xb, sem, *, TM):
    t = pl.program_id(0)
    n_used = nu_ref[0]
    used = t < n_used

    def start_gather(idx_ref):
        def issue(r, carry):
            _row_copy(h_hbm, idx_ref[0, 0, r], xf, r, sem).start()
            return carry

        lax.fori_loop(0, TM, issue, 0, unroll=8)

    @pl.when(used & (pl.program_id(1) == 0))
    def _():
        @pl.when(t == 0)
        def _():
            start_gather(src_ref)

        pltpu.make_async_copy(h_hbm.at[pl.ds(0, TM), :], xf, sem).wait()
        xb[...] = xf[...].astype(BF16)

        @pl.when(t + 1 < n_used)
        def _():
            start_gather(nxt_ref)

    @pl.when(used)
    def _():
        x = xb[...]
        g = jnp.dot(x, wg_ref[...].astype(BF16), preferred_element_type=F32)
        u = jnp.dot(x, wu_ref[...].astype(BF16), preferred_element_type=F32)
        o_ref[...] = (g * (1.0 / (1.0 + jnp.exp(-g))) * u).astype(o_ref.dtype)

    @pl.when(jnp.logical_not(used))
    def _():
        o_ref[...] = jnp.zeros_like(o_ref)


def moe_gate_up(h, wg, wu, layer, plan, *, TM, tn=256):
    N, K = h.shape
    F = wg.shape[-1]
    T = plan["T"]
    tn = _pick(F, tn)
    nj = F // tn
    wspec = pl.BlockSpec((None, None, K, tn),
                         lambda t, j, te, nu: (layer, te[t], 0, jnp.where(t < nu[0], j, nj - 1)))
    grid_spec = pltpu.PrefetchScalarGridSpec(
        num_scalar_prefetch=2,
        grid=(T, nj),
        in_specs=[pl.BlockSpec((1, 1, TM), lambda t, j, te, nu: (t, 0, 0), memory_space=pltpu.SMEM),
                  pl.BlockSpec((1, 1, TM), lambda t, j, te, nu: (jnp.minimum(t + 1, T - 1), 0, 0),
                               memory_space=pltpu.SMEM),
                  pl.BlockSpec(memory_space=pl.ANY), wspec, wspec],
        out_specs=pl.BlockSpec((TM, tn), lambda t, j, te, nu: (t, j)),
        scratch_shapes=[pltpu.VMEM((TM, K), F32), pltpu.VMEM((TM, K), BF16), pltpu.SemaphoreType.DMA],
    )
    return pl.pallas_call(
        functools.partial(_moe_gate_up_kernel, TM=TM),
        out_shape=jax.ShapeDtypeStruct((T * TM, F), BF16),
        grid_spec=grid_spec,
        compiler_params=_cparams(("arbitrary", "arbitrary")),
        name="moe_gate_up",
    )(plan["tile_expert"], plan["n_used"], plan["src"], plan["src"], h, wg, wu)


def _moe_down_kernel(te_ref, nu_ref, a_ref, w_ref, rg_ref, o_ref):
    used = pl.program_id(0) < nu_ref[0]

    @pl.when(used)
    def _():
        o_ref[...] = rg_ref[...] * jnp.dot(a_ref[...], w_ref[...].astype(BF16), preferred_element_type=F32)

    @pl.when(jnp.logical_not(used))
    def _():
        o_ref[...] = jnp.zeros_like(o_ref)


def moe_down(a, wd, layer, plan, *, TM, tn=512):
    P, F = a.shape
    D = wd.shape[-1]
    T = plan["T"]
    tn = _pick(D, tn)
    nj = D // tn
    grid_spec = pltpu.PrefetchScalarGridSpec(
        num_scalar_prefetch=2,
        grid=(T, nj),
        in_specs=[pl.BlockSpec((TM, F), lambda t, j, te, nu: (t, 0)),
                  pl.BlockSpec((None, None, F, tn),
                               lambda t, j, te, nu: (layer, te[t], 0, jnp.where(t < nu[0], j, nj - 1))),
                  pl.BlockSpec((TM, 1), lambda t, j, te, nu: (t, 0))],
        out_specs=pl.BlockSpec((TM, tn), lambda t, j, te, nu: (t, j)),
    )
    return pl.pallas_call(
        _moe_down_kernel,
        out_shape=jax.ShapeDtypeStruct((P, D), F32),
        grid_spec=grid_spec,
        compiler_params=_cparams(("arbitrary", "arbitrary")),
        name="moe_down",
    )(plan["tile_expert"], plan["n_used"], a, wd, plan["row_gate"])


def _moe_combine_kernel(rows_ref, ye_hbm, o_ref, buf, sem, *, tm):
    def issue(r, carry):
        for c in range(2):
            _row_copy(ye_hbm, rows_ref[0, c, r], buf, c * tm + r, sem).start()
        return carry

    lax.fori_loop(0, tm, issue, 0, unroll=8)
    pltpu.make_async_copy(ye_hbm.at[pl.ds(0, 2 * tm), :], buf, sem).wait()
    o_ref[...] = buf[0:tm, :] + buf[tm:2 * tm, :]


def moe_combine(ye, plan, N, *, tm=256):
    D = ye.shape[1]
    tm = _pick(N, tm)
    rows = plan["rows"].reshape(2, N // tm, tm).transpose(1, 0, 2)
    return pl.pallas_call(
        functools.partial(_moe_combine_kernel, tm=tm),
        out_shape=jax.ShapeDtypeStruct((N, D), F32),
        grid=(N // tm,),
        in_specs=[pl.BlockSpec((1, 2, tm), lambda i: (i, 0, 0), memory_space=pltpu.SMEM),
                  pl.BlockSpec(memory_space=pl.ANY)],
        out_specs=pl.BlockSpec((tm, D), lambda i: (i, 0)),
        scratch_shapes=[pltpu.VMEM((2 * tm, D), F32), pltpu.SemaphoreType.DMA],
        compiler_params=_cparams(("arbitrary",)),
        name="moe_combine",
    )(rows, ye)


def _ln(v):
    mu = jnp.mean(v, axis=-1, keepdims=True)
    c = v - mu
    var = jnp.mean(c * c, axis=-1, keepdims=True)
    return c * lax.rsqrt(var + LN_EPS)


def _ln_mod_kernel(x_ref, sc_ref, sh_ref, h_ref):
    h_ref[0] = (_ln(x_ref[0]) * (1.0 + sc_ref[0]) + sh_ref[0]).astype(h_ref.dtype)


def ln_mod(x, sc, sh, *, tm=256):
    B, S, D = x.shape
    tm = _pick(S, tm)
    row = pl.BlockSpec((1, tm, D), lambda b, i: (b, i, 0))
    vec = pl.BlockSpec((1, 1, D), lambda b, i: (b, 0, 0))
    return pl.pallas_call(
        _ln_mod_kernel,
        out_shape=jax.ShapeDtypeStruct((B, S, D), BF16),
        grid=(B, S // tm),
        in_specs=[row, vec, vec],
        out_specs=row,
        compiler_params=_cparams(("parallel", "parallel")),
        name="ln_mod",
    )(x, sc, sh)


def _split3(v):
    hi = v.astype(BF16)
    r1 = v - hi.astype(F32)
    mid = r1.astype(BF16)
    lo = (r1 - mid.astype(F32)).astype(BF16)
    return hi, mid, lo


def _res_ln_kernel(*refs, alpha, with_h, with_router):
    it = iter(refs)
    x_ref, y_ref, g_ref, lng_ref, lnb_ref = (next(it) for _ in range(5))
    sc_ref = sh_ref = rw_ref = rb_ref = None
    if with_h:
        sc_ref, sh_ref = next(it), next(it)
    if with_router:
        rw_ref, rb_ref = next(it), next(it)
    xo_ref = next(it)
    h_ref = next(it) if with_h else None
    gate_ref = next(it) if with_router else None

    v = alpha * x_ref[0] + (1.0 + g_ref[0]) * y_ref[0]
    xn = _ln(v) * lng_ref[...] + lnb_ref[...]
    xo_ref[0] = xn
    if not with_h:
        return
    h = _ln(xn) * (1.0 + sc_ref[0]) + sh_ref[0]
    h_ref[0] = h.astype(h_ref.dtype)
    if not with_router:
        return
    hs = _split3(h)
    ws = _split3(rw_ref[...])
    logits = rb_ref[...]
    for a in range(3):
        for b in range(3 - a):
            logits = logits + jnp.dot(hs[a], ws[b], preferred_element_type=F32)
    lane = lax.broadcasted_iota(jnp.int32, logits.shape, 1).astype(F32)
    logits = jnp.where(lane < N_EXPERTS, logits, LOWEST)
    v1 = jnp.max(logits, axis=-1, keepdims=True)
    i1 = jnp.min(jnp.where(logits == v1, lane, float(LANES)), axis=-1, keepdims=True)
    rest = jnp.where(lane == i1, LOWEST, logits)
    v2 = jnp.max(rest, axis=-1, keepdims=True)
    i2 = jnp.min(jnp.where(rest == v2, lane, float(LANES)), axis=-1, keepdims=True)
    e2 = jnp.exp(v2 - v1)
    den = 1.0 + e2
    gate_ref[0] = jnp.where(lane == 0.0, i1, jnp.where(lane == 1.0, i2,
                            jnp.where(lane == 2.0, 1.0 / den, jnp.where(lane == 3.0, e2 / den, 0.0))))


def res_ln(x, y, g, lng, lnb, alpha, *, sc=None, sh=None, router_w=None, router_b=None, tm=256):
    B, S, D = x.shape
    tm = _pick(S, tm)
    with_h = sc is not None
    with_router = router_w is not None
    row = pl.BlockSpec((1, tm, D), lambda b, i: (b, i, 0))
    vec = pl.BlockSpec((1, 1, D), lambda b, i: (b, 0, 0))
    par = pl.BlockSpec((1, D), lambda b, i: (0, 0))
    in_specs = [row, row, vec, par, par]
    args = [x, y, g, lng, lnb]
    out_shape = [jax.ShapeDtypeStruct((B, S, D), F32)]
    out_specs = [row]
    if with_h:
        in_specs += [vec, vec]
        args += [sc, sh]
        out_shape.append(jax.ShapeDtypeStruct((B, S, D), F32 if with_router else BF16))
        out_specs.append(row)
    if with_router:
        in_specs += [pl.BlockSpec((D, LANES), lambda b, i: (0, 0)), pl.BlockSpec((1, LANES), lambda b, i: (0, 0))]
        args += [router_w, router_b]
        out_shape.append(jax.ShapeDtypeStruct((B, S, LANES), F32))
        out_specs.append(pl.BlockSpec((1, tm, LANES), lambda b, i: (b, i, 0)))
    return pl.pallas_call(
        functools.partial(_res_ln_kernel, alpha=alpha, with_h=with_h, with_router=with_router),
        out_shape=out_shape,
        grid=(B, S // tm),
        in_specs=in_specs,
        out_specs=out_specs,
        compiler_params=_cparams(("parallel", "parallel")),
        name="res_ln",
    )(*args)


def _rms(v, g):
    return (v * lax.rsqrt(jnp.mean(v * v, axis=-1, keepdims=True) + LN_EPS) * g).astype(BF16)


def _out_proj_kernel(oa_ref, on_ref, oc_ref, ga_ref, gn_ref, gc_ref, wa_ref, wn_ref, wc_ref, o_ref,
                     ya_sc, yn_sc, yc_sc):
    @pl.when(pl.program_id(1) == 0)
    def _():
        ya_sc[...] = _rms(oa_ref[...], ga_ref[...])
        yn_sc[...] = _rms(on_ref[...], gn_ref[...])
        yc_sc[...] = _rms(oc_ref[...], gc_ref[...])

    acc = jnp.dot(ya_sc[...], wa_ref[...].astype(BF16), preferred_element_type=F32)
    acc += jnp.dot(yn_sc[...], wn_ref[...].astype(BF16), preferred_element_type=F32)
    acc += jnp.dot(yc_sc[...], wc_ref[...].astype(BF16), preferred_element_type=F32)
    o_ref[...] = acc


def out_proj(oa, on, oc, mix_g, w_out, layer, *, tm=1024, tn=512):
    M = oa.shape[0]
    C = oc.shape[1]
    D = w_out.shape[-1]
    tm = _pick(M, tm)
    tn = _pick(D, tn)
    assert A_W == N_W and (A_W + N_W) % C == 0
    ga = mix_g[layer:layer + 1, :A_W]
    gn = mix_g[layer:layer + 1, A_W:A_W + N_W]
    gc = mix_g[layer:layer + 1, A_W + N_W:]
    return pl.pallas_call(
        _out_proj_kernel,
        out_shape=jax.ShapeDtypeStruct((M, D), F32),
        grid=(M // tm, D // tn),
        in_specs=[pl.BlockSpec((tm, A_W), lambda i, j: (i, 0), pipeline_mode=pl.Buffered(1)),
                  pl.BlockSpec((tm, N_W), lambda i, j: (i, 0), pipeline_mode=pl.Buffered(1)),
                  pl.BlockSpec((tm, C), lambda i, j: (i, 0), pipeline_mode=pl.Buffered(1)),
                  pl.BlockSpec((1, A_W), lambda i, j: (0, 0)),
                  pl.BlockSpec((1, N_W), lambda i, j: (0, 0)),
                  pl.BlockSpec((1, C), lambda i, j: (0, 0)),
                  pl.BlockSpec((None, A_W, tn), lambda i, j: (layer, 0, j)),
                  pl.BlockSpec((None, N_W, tn), lambda i, j: (layer, 1, j)),
                  pl.BlockSpec((None, C, tn), lambda i, j: (layer, (A_W + N_W) // C, j))],
        out_specs=pl.BlockSpec((tm, tn), lambda i, j: (i, j)),
        scratch_shapes=[pltpu.VMEM((tm, A_W), BF16), pltpu.VMEM((tm, N_W), BF16), pltpu.VMEM((tm, C), BF16)],
        compiler_params=_cparams(("parallel", "arbitrary")),
        name="out_proj",
    )(oa, on, oc, ga, gn, gc, w_out, w_out, w_out)


def _conv_kernel(cur_ref, prev_ref, w_ref, o_ref, *, C):
    i = pl.program_id(1)
    cur = cur_ref[0]
    u, bg, cg = (cur[:, GATE_W + k * C:GATE_W + (k + 1) * C] for k in range(3))
    z = cg * u
    prev = prev_ref[0]
    zp = prev[:, GATE_W + 2 * C:GATE_W + 3 * C] * prev[:, GATE_W:GATE_W + C]
    zp = jnp.where(i > 0, zp, 0.0)
    rows = lax.broadcasted_iota(jnp.int32, z.shape, 0)
    z1 = jnp.where(rows >= 1, pltpu.roll(z, 1, axis=0), zp[7:8, :])
    z2 = jnp.where(rows >= 2, pltpu.roll(z, 2, axis=0), jnp.where(rows == 1, zp[7:8, :], zp[6:7, :]))
    w = w_ref[...]
    o_ref[0] = bg * (w[0:1, :] * z2 + w[1:2, :] * z1 + w[2:3, :] * z)


def short_conv(pb, conv_w_l, *, tm=512):
    B, S, W = pb.shape
    C = conv_w_l.shape[-1]
    tm = _pick(S, tm)
    return pl.pallas_call(
        functools.partial(_conv_kernel, C=C),
        out_shape=jax.ShapeDtypeStruct((B, S, C), F32),
        grid=(B, S // tm),
        in_specs=[pl.BlockSpec((1, tm, W), lambda b, i: (b, i, 0)),
                  pl.BlockSpec((1, 8, W), lambda b, i: (b, jnp.maximum(i * (tm // 8) - 1, 0), 0)),
                  pl.BlockSpec((CONV_W, C), lambda b, i: (0, 0))],
        out_specs=pl.BlockSpec((1, tm, C), lambda b, i: (b, i, 0)),
        compiler_params=_cparams(("parallel", "parallel")),
        name="short_conv",
    )(pb, pb, conv_w_l)


def _dilated_kernel(sl_ref, q_ref, k_ref, v_ref, o_ref, qf, kf, vf, u_sc, m_sc, l_sc):
    h = pl.program_id(1)
    sb = pl.program_id(2)
    base = sb * A_SUPER

    @pl.when(sb == 0)
    def _():
        kf[...] = k_ref[0].astype(F32)
        vf[...] = v_ref[0].astype(F32)

    qf[...] = q_ref[0].astype(F32)
    slope = sl_ref[h]
    qi = lax.broadcasted_iota(jnp.int32, (A_BLK, A_BLK), 0)
    kj = lax.broadcasted_iota(jnp.int32, (A_BLK, A_BLK), 1)
    diff = (qi - kj).astype(F32)
    cur_ok = kj <= qi
    prev_band = kj >= qi
    nt = (((1,), (1,)), ((), ()))

    for br, (_, d) in enumerate(A_BRANCHES):
        span = A_BLK * d
        sd = slope * float(d)
        bias_c = sd * diff
        bias_p = sd * (diff + float(A_BLK))

        def ds(start):
            return pl.ds(start, A_BLK, stride=d) if d > 1 else pl.ds(start, A_BLK)

        for grp in range(A_SUPER // A_BLK // A_GROUP):
            blocks = []
            for i in range(grp * A_GROUP, (grp + 1) * A_GROUP):
                loc = (i // d) * span + (i % d)
                has_prev = (sb > 0) if loc < span else True
                start = base + loc
                pstart = jnp.where(has_prev, start - span, start) if loc < span else start - span
                blocks.append((ds(loc), ds(start), ds(pstart), has_prev))
            s_c, s_p, hp = [], [], []
            for lrow, rows, prow, has_prev in blocks:
                q = qf[lrow, :].astype(BF16)
                s_c.append(lax.dot_general(q, kf[rows, :].astype(BF16), nt, preferred_element_type=F32))
                s_p.append(lax.dot_general(q, kf[prow, :].astype(BF16), nt, preferred_element_type=F32))
                hp.append(jnp.full((1, 1), jnp.where(has_prev, 0.0, NEG), F32))
            s_c = jnp.where(cur_ok[None], jnp.stack(s_c) - bias_c[None], NEG)
            s_p = jnp.where(prev_band[None], jnp.stack(s_p) - bias_p[None] + jnp.stack(hp), NEG)
            m = jnp.max(jnp.maximum(s_c, s_p), axis=-1, keepdims=True)
            e_c = jnp.exp(s_c - m)
            e_p = jnp.exp(s_p - m)
            l = jnp.sum(e_c + e_p, axis=-1, keepdims=True)
            for n, (lrow, rows, prow, _) in enumerate(blocks):
                u = jnp.dot(e_c[n].astype(BF16), vf[rows, :].astype(BF16), preferred_element_type=F32)
                u += jnp.dot(e_p[n].astype(BF16), vf[prow, :].astype(BF16), preferred_element_type=F32)
                u_sc[br, lrow, :] = u
                m_sc[br, lrow, :] = jnp.broadcast_to(m[n], (A_BLK, HEAD_DIM))
                l_sc[br, lrow, :] = jnp.broadcast_to(l[n], (A_BLK, HEAD_DIM))

    nb = len(A_BRANCHES)
    mt = m_sc[0]
    for br in range(1, nb):
        mt = jnp.maximum(mt, m_sc[br])
    num = jnp.zeros((A_SUPER, HEAD_DIM), F32)
    den = jnp.zeros((A_SUPER, HEAD_DIM), F32)
    for br in range(nb):
        w = jnp.exp(m_sc[br] - mt)
        num += w * u_sc[br]
        den += w * l_sc[br]
    o_ref[0] = num / den


def dilated_attention(pa, slopes):
    B, S, _ = pa.shape
    assert S % A_SUPER == 0
    grid_spec = pltpu.PrefetchScalarGridSpec(
        num_scalar_prefetch=1,
        grid=(B, A_HEADS, S // A_SUPER),
        in_specs=[pl.BlockSpec((1, A_SUPER, HEAD_DIM), lambda b, h, s, sl: (b, s, h)),
                  pl.BlockSpec((1, S, HEAD_DIM), lambda b, h, s, sl: (b, 0, A_HEADS + h)),
                  pl.BlockSpec((1, S, HEAD_DIM), lambda b, h, s, sl: (b, 0, 2 * A_HEADS + h))],
        out_specs=pl.BlockSpec((1, A_SUPER, HEAD_DIM), lambda b, h, s, sl: (b, s, h)),
        scratch_shapes=[pltpu.VMEM((A_SUPER, HEAD_DIM), F32),
                        pltpu.VMEM((S, HEAD_DIM), F32),
                        pltpu.VMEM((S, HEAD_DIM), F32),
                        pltpu.VMEM((len(A_BRANCHES), A_SUPER, HEAD_DIM), F32),
                        pltpu.VMEM((len(A_BRANCHES), A_SUPER, HEAD_DIM), F32),
                        pltpu.VMEM((len(A_BRANCHES), A_SUPER, HEAD_DIM), F32)],
    )
    return pl.pallas_call(
        _dilated_kernel,
        out_shape=jax.ShapeDtypeStruct((B, S, A_W), F32),
        grid_spec=grid_spec,
        compiler_params=_cparams(("parallel", "parallel", "arbitrary")),
        name="dilated_attention",
    )(slopes, pa, pa, pa)


def _compress_kernel(kc_ref, vc_ref, wk_ref, wv_ref, pe_ref, ko_ref, vo_ref, xf, bsc, *, nc):
    for x_ref, w_ref, o_ref in ((kc_ref, wk_ref, ko_ref), (vc_ref, wv_ref, vo_ref)):
        xf[...] = x_ref[0].astype(F32)
        first = jnp.zeros((nc, HEAD_DIM), F32)
        second = jnp.zeros((nc, HEAD_DIM), F32)
        for j in range(CMP_STRIDE):
            xj = xf[pl.ds(j, nc, stride=CMP_STRIDE), :]
            wa = w_ref[j * HEAD_DIM:(j + 1) * HEAD_DIM, :].astype(BF16)
            wb = w_ref[(CMP_STRIDE + j) * HEAD_DIM:(CMP_STRIDE + j + 1) * HEAD_DIM, :].astype(BF16)
            first += jnp.dot((xj + pe_ref[j:j + 1, :]).astype(BF16), wa, preferred_element_type=F32)
            second += jnp.dot((xj + pe_ref[CMP_STRIDE + j:CMP_STRIDE + j + 1, :]).astype(BF16), wb,
                              preferred_element_type=F32)
        bsc[0:nc, :] = second
        bsc[nc:nc + 8, :] = jnp.zeros((8, HEAD_DIM), F32)
        out = first + bsc[1:nc + 1, :]
        row = lax.broadcasted_iota(jnp.int32, out.shape, 0)
        o_ref[0, 0] = jnp.where(row < nc - 1, out, 0.0).astype(o_ref.dtype)


def nsa_compress(pa, cmp_wk, cmp_wv, cmp_pe, layer):
    B, S, _ = pa.shape
    nc = S // CMP_STRIDE
    kc0 = (3 * A_W + N_W) // HEAD_DIM
    vc0 = kc0 + NSA_KV_HEADS
    wspec = pl.BlockSpec((None, CMP_LEN * HEAD_DIM, HEAD_DIM), lambda b, g: (layer, 0, 0))
    ospec = pl.BlockSpec((1, 1, nc, HEAD_DIM), lambda b, g: (b, g, 0, 0))
    oshape = jax.ShapeDtypeStruct((B, NSA_KV_HEADS, nc, HEAD_DIM), BF16)
    return pl.pallas_call(
        functools.partial(_compress_kernel, nc=nc),
        out_shape=[oshape, oshape],
        grid=(B, NSA_KV_HEADS),
        in_specs=[pl.BlockSpec((1, S, HEAD_DIM), lambda b, g: (b, 0, kc0 + g)),
                  pl.BlockSpec((1, S, HEAD_DIM), lambda b, g: (b, 0, vc0 + g)),
                  wspec, wspec,
                  pl.BlockSpec((None, CMP_LEN, HEAD_DIM), lambda b, g: (layer, 0, 0))],
        out_specs=[ospec, ospec],
        scratch_shapes=[pltpu.VMEM((S, HEAD_DIM), F32), pltpu.VMEM((nc + 8, HEAD_DIM), F32)],
        compiler_params=_cparams(("parallel", "parallel")),
        name="nsa_compress",
    )(pa, pa, cmp_wk, cmp_wv, cmp_pe)


def _gate_cols(gt_ref, g, M):
    sig = 1.0 / (1.0 + jnp.exp(-gt_ref[0]))
    lane = lax.broadcasted_iota(jnp.int32, sig.shape, 1)
    return [[jnp.sum(jnp.where(lane == (g * M + m) * 3 + b, sig, 0.0), axis=-1, keepdims=True)
             for b in range(3)] for m in range(M)]


def _nsa_select_kernel(sl_ref, q_ref, gt_ref, kc_ref, vc_ref, op_ref, mn_ref, *, TQ, nc):
    g = pl.program_id(1)
    q0 = pl.program_id(2) * TQ
    M = NSA_GROUP
    nt = (((1,), (1,)), ((), ()))

    q = q_ref[0]
    qs = jnp.concatenate([q[:, m * HEAD_DIM:(m + 1) * HEAD_DIM] for m in range(M)], axis=0)
    slope_col = jnp.concatenate([jnp.full((TQ, 1), sl_ref[g * M + m], F32) for m in range(M)], axis=0)
    tq_i = q0 + lax.broadcasted_iota(jnp.int32, (TQ, 1), 0)
    t_col = jnp.concatenate([tq_i] * M, axis=0).astype(F32)

    def softmax_rows(s, ok):
        s = jnp.where(ok, s, NEG)
        m = jnp.max(s, axis=-1, keepdims=True)
        e = jnp.where(ok, jnp.exp(s - m), 0.0)
        den = jnp.sum(e, axis=-1, keepdims=True)
        return e / jnp.maximum(den, TINY)

    cidx = lax.broadcasted_iota(jnp.int32, (1, nc), 1)
    c_end = (cidx * CMP_STRIDE + (CMP_LEN - 1)).astype(F32)
    dist = t_col - c_end
    s = lax.dot_general(qs, kc_ref[0, 0], nt, preferred_element_type=F32) - slope_col * dist
    p_cmp = softmax_rows(s, (dist >= 0) & (cidx < nc - 1))
    o_cmp = jnp.dot(p_cmp.astype(BF16), vc_ref[0, 0], preferred_element_type=F32)

    psum = p_cmp[0:TQ]
    for m in range(1, M):
        psum = psum + p_cmp[m * TQ:(m + 1) * TQ]
    ci = lax.broadcasted_iota(jnp.int32, (nc, LANES), 0)
    ni = lax.broadcasted_iota(jnp.int32, (nc, LANES), 1)
    overlap = ((ci * CMP_STRIDE < ni * SEL_BLK + SEL_BLK) & (ci * CMP_STRIDE + CMP_LEN - 1 >= ni * SEL_BLK)
               & (ci < nc - 1))
    overlap = jnp.where(overlap, 1.0, 0.0).astype(BF16)
    imp = jnp.zeros((TQ, LANES), F32)
    for piece in _split3(psum):
        imp = imp + jnp.dot(piece, overlap, preferred_element_type=F32)
    jb = lax.broadcasted_iota(jnp.int32, (TQ, LANES), 1)
    cur = tq_i // SEL_BLK
    forced = (jb == 0) | (jb == cur) | (jb == cur - 1)
    imp = jnp.where(forced, imp + FORCE, imp)
    causal_blk = jb * SEL_BLK <= tq_i
    imp = jnp.where(causal_blk, imp, NEG)

    work = imp
    memb = jnp.zeros((TQ, LANES), F32)
    jbf = jb.astype(F32)
    for _ in range(TOP_N):
        mx = jnp.max(work, axis=-1, keepdims=True)
        first = jnp.min(jnp.where(work == mx, jbf, float(LANES)), axis=-1, keepdims=True)
        pick = jbf == first
        memb = jnp.where(pick, 1.0, memb)
        work = jnp.where(pick, LOWEST, work)
    chosen = (memb > 0.5) & causal_blk
    mn_ref[0, 0] = jnp.where(chosen, 0.0, NEG).astype(mn_ref.dtype)
    gates = _gate_cols(gt_ref, g, M)
    for m in range(M):
        op_ref[0, :, m * HEAD_DIM:(m + 1) * HEAD_DIM] = gates[m][0] * o_cmp[m * TQ:(m + 1) * TQ]


def nsa_select(pa, pb, k_cmp, v_cmp, slopes, *, TQ=512):
    B, S, _ = pa.shape
    nc = k_cmp.shape[2]
    M = NSA_GROUP
    TQ = _pick(S, TQ)
    assert S // SEL_BLK <= LANES
    q0 = 3 * A_W // (M * HEAD_DIM)
    cmp_spec = pl.BlockSpec((1, 1, nc, HEAD_DIM), lambda b, g, t, sl: (b, g, 0, 0))
    grid_spec = pltpu.PrefetchScalarGridSpec(
        num_scalar_prefetch=1,
        grid=(B, NSA_KV_HEADS, S // TQ),
        in_specs=[pl.BlockSpec((1, TQ, M * HEAD_DIM), lambda b, g, t, sl: (b, t, q0 + g)),
                  pl.BlockSpec((1, TQ, LANES), lambda b, g, t, sl: (b, t, 0)),
                  cmp_spec, cmp_spec],
        out_specs=[pl.BlockSpec((1, TQ, M * HEAD_DIM), lambda b, g, t, sl: (b, t, g)),
                   pl.BlockSpec((1, 1, TQ, LANES), lambda b, g, t, sl: (b, g, t, 0))],
    )
    return pl.pallas_call(
        functools.partial(_nsa_select_kernel, TQ=TQ, nc=nc),
        out_shape=[jax.ShapeDtypeStruct((B, S, N_W), F32),
                   jax.ShapeDtypeStruct((B, NSA_KV_HEADS, S, LANES), BF16)],
        grid_spec=grid_spec,
        compiler_params=_cparams(("parallel", "parallel", "parallel")),
        name="nsa_select",
    )(slopes, pa, pb, k_cmp, v_cmp)


def _nsa_sweep_kernel(sl_ref, q_ref, gt_ref, op_ref, mn_ref, ks_ref, vs_ref, kw_ref, vw_ref, o_ref,
                      qa_sc, mx_sc, l_sc, acc_sc, s_sc, flag_ref, act_ref, *, TQ, n_kt, G):
    g = pl.program_id(1)
    qt = pl.program_id(2)
    M = NSA_GROUP
    R = M * TQ
    kt = TQ
    bpk = kt // SEL_BLK
    nt = (((1,), (1,)), ((), ()))

    q = q_ref[0]
    mn = mn_ref[0, 0]
    for m in range(M):
        qa_sc[m * TQ:(m + 1) * TQ, 0:HEAD_DIM] = q[:, m * HEAD_DIM:(m + 1) * HEAD_DIM]
        qa_sc[m * TQ:(m + 1) * TQ, HEAD_DIM:2 * HEAD_DIM] = mn
    slope_rep = jnp.concatenate([jnp.full((TQ, kt), sl_ref[g * M + m], F32) for m in range(M)], axis=0)
    trel = jnp.concatenate([lax.broadcasted_iota(jnp.int32, (TQ, 1), 0)] * M, axis=0)
    lane = lax.broadcasted_iota(jnp.int32, (1, kt), 1)

    cnt = jnp.sum(jnp.where(mn.astype(F32) == 0.0, 1.0, 0.0), axis=0, keepdims=True)
    cnt = jnp.broadcast_to(cnt, (8, LANES))
    per_tile = cnt
    for b in range(1, bpk):
        per_tile = per_tile + pltpu.roll(cnt, LANES - b, axis=1)
    for j in range(n_kt):
        flag_ref[j] = (per_tile[0, j * bpk] > 0.0).astype(jnp.int32)

    act_ref[0] = 0

    def compact(j, n):
        act_ref[n] = j
        return n + flag_ref[j]

    n_act = lax.fori_loop(0, qt, compact, 0)
    for u in range(1, G):
        act_ref[n_act + u - 1] = act_ref[0]
    n_grp = (n_act + G - 1) // G

    def rows_of(j):
        return pl.ds(pl.multiple_of(j * kt, kt), kt)

    def sel_scores(j, shift):
        ki = lax.broadcasted_iota(jnp.int32, (kt, LANES), 0)
        bi = lax.broadcasted_iota(jnp.int32, (kt, LANES), 1)
        onehot = jnp.where(bi == j * bpk + ki // SEL_BLK, 1.0, 0.0).astype(BF16)
        s = lax.dot_general(qa_sc[...], jnp.concatenate([ks_ref[0, rows_of(j), :], onehot], axis=1), nt,
                            preferred_element_type=F32)
        return s + slope_rep * (((j - qt) * kt + lane).astype(F32) + shift)

    mx_sc[...] = jnp.full((R, kt), 0.5 * NEG, F32)

    def max_body(gi, carry):
        m = mx_sc[...]
        for u in range(G):
            idx = gi * G + u
            s = sel_scores(act_ref[idx], jnp.where(idx < n_act, 0.0, -1e34))
            s_sc[idx] = s
            m = jnp.maximum(m, s)
        mx_sc[...] = m
        return carry

    lax.fori_loop(0, n_grp, max_body, 0)
    s_diag = jnp.where(lane <= trel, sel_scores(qt, 0.0), NEG)
    m_row = jnp.max(jnp.maximum(mx_sc[...], s_diag), axis=-1, keepdims=True)
    mx_sc[...] = jnp.broadcast_to(m_row, (R, kt))

    p = jnp.exp(s_diag - m_row)
    l_sc[...] = p
    acc_sc[...] = jnp.dot(p.astype(BF16), vs_ref[0, rows_of(qt), :], preferred_element_type=F32)

    def acc_body(gi, carry):
        m = mx_sc[...]
        l = l_sc[...]
        ps, vs = [], []
        for u in range(G):
            idx = gi * G + u
            p = jnp.exp(s_sc[idx] - m)
            l = l + p
            ps.append(p.astype(BF16))
            vs.append(vs_ref[0, rows_of(act_ref[idx]), :])
        l_sc[...] = l
        acc_sc[...] += jnp.dot(jnp.concatenate(ps, axis=1), jnp.concatenate(vs, axis=0), preferred_element_type=F32)
        return carry

    lax.fori_loop(0, n_grp, acc_body, 0)
    o_sel = acc_sc[...] / jnp.maximum(jnp.sum(l_sc[...], axis=-1, keepdims=True), TINY)

    n_w = NSA_WIN // kt + 1
    s_w, v_w = [], []
    for u in range(n_w):
        jc = jnp.maximum(qt - (n_w - 1) + u, 0)
        prel = (u - (n_w - 1)) * kt + lane
        d = trel - prel
        ok = (d >= 0) & (d < NSA_WIN) & (prel >= -qt * kt)
        s = lax.dot_general(qa_sc[:, 0:HEAD_DIM], kw_ref[0, rows_of(jc), :], nt, preferred_element_type=F32)
        s_w.append(jnp.where(ok, s + slope_rep * prel.astype(F32), NEG))
        v_w.append(vw_ref[0, rows_of(jc), :])
    m_el = s_w[0]
    for s in s_w[1:]:
        m_el = jnp.maximum(m_el, s)
    m_row = jnp.max(m_el, axis=-1, keepdims=True)
    p_w = [jnp.exp(s - m_row) for s in s_w]
    l_el = p_w[0]
    for p in p_w[1:]:
        l_el = l_el + p
    o_win = jnp.dot(jnp.concatenate([p.astype(BF16) for p in p_w], axis=1), jnp.concatenate(v_w, axis=0),
                    preferred_element_type=F32) / jnp.maximum(jnp.sum(l_el, axis=-1, keepdims=True), TINY)

    gates = _gate_cols(gt_ref, g, M)
    for m in range(M):
        rs = slice(m * TQ, (m + 1) * TQ)
        cs = slice(m * HEAD_DIM, (m + 1) * HEAD_DIM)
        o_ref[0, :, cs] = op_ref[0, :, cs] + gates[m][1] * o_sel[rs] + gates[m][2] * o_win[rs]


def nsa_sweep(pa, pb, o_part, mneg, slopes, *, TQ=128, group=4):
    B, S, _ = pa.shape
    M = NSA_GROUP
    assert TQ == LANES and TQ % SEL_BLK == 0 and NSA_WIN % TQ == 0 and S % TQ == 0
    q0 = 3 * A_W // (M * HEAD_DIM)
    kv0 = (3 * A_W + N_W) // HEAD_DIM + 2 * NSA_KV_HEADS
    kv = lambda n: pl.BlockSpec((1, S, HEAD_DIM), lambda b, g, t, sl: (b, 0, kv0 + n * NSA_KV_HEADS + g))
    qspec = pl.BlockSpec((1, TQ, M * HEAD_DIM), lambda b, g, t, sl: (b, t, q0 + g))
    ospec = pl.BlockSpec((1, TQ, M * HEAD_DIM), lambda b, g, t, sl: (b, t, g))
    grid_spec = pltpu.PrefetchScalarGridSpec(
        num_scalar_prefetch=1,
        grid=(B, NSA_KV_HEADS, S // TQ),
        in_specs=[qspec,
                  pl.BlockSpec((1, TQ, LANES), lambda b, g, t, sl: (b, t, 0)),
                  ospec,
                  pl.BlockSpec((1, 1, TQ, LANES), lambda b, g, t, sl: (b, g, t, 0)),
                  kv(0), kv(1), kv(2), kv(3)],
        out_specs=ospec,
        scratch_shapes=[pltpu.VMEM((M * TQ, 2 * HEAD_DIM), BF16),
                        pltpu.VMEM((M * TQ, TQ), F32), pltpu.VMEM((M * TQ, TQ), F32),
                        pltpu.VMEM((M * TQ, HEAD_DIM), F32),
                        pltpu.VMEM((S // TQ + group, M * TQ, TQ), F32),
                        pltpu.SMEM((S // TQ,), jnp.int32),
                        pltpu.SMEM((S // TQ + group,), jnp.int32)],
    )
    return pl.pallas_call(
        functools.partial(_nsa_sweep_kernel, TQ=TQ, n_kt=S // TQ, G=group),
        out_shape=jax.ShapeDtypeStruct((B, S, N_W), F32),
        grid_spec=grid_spec,
        compiler_params=_cparams(("parallel", "parallel", "arbitrary")),
        name="nsa_sweep",
    )(slopes, pa, pb, o_part, mneg, pa, pa, pa, pa)


def kernel(x, c, ada_w, ada_b, w_in, conv_w, cmp_wk, cmp_wv, cmp_pe, mix_g, w_out, ln1_g, ln1_b, ln2_g, ln2_b,
           ffn_w_gate, ffn_w_up, ffn_w_down, moe_router, moe_router_b, moe_w_gate, moe_w_up, moe_w_down):
    B, S, D = x.shape
    depth = ada_w.shape[0]
    C = conv_w.shape[-1]
    N = B * S
    alpha = (2 * depth) ** 0.25
    assert D == A_W + N_W + C and w_in.shape[-1] == ATT_W + GATE_W + 3 * C and ATT_W % PB_TN == 0
    pb_w = pl.cdiv(GATE_W + 3 * C, PB_TN) * PB_TN
    w_in_b = repack_weight(jnp.transpose(w_in, (2, 0, 1)), ATT_W + pb_w, tn=PB_TN)

    n_heads = A_HEADS + NSA_HEADS
    sl = jnp.exp2(-8.0 * jnp.arange(1, n_heads + 1, dtype=F32) / n_heads)
    sl_a, sl_n = sl[0::2], sl[1::2]
    scale = HEAD_DIM ** -0.5
    col = jnp.arange(ATT_W)
    is_q = (col < A_W) | ((col >= 3 * A_W) & (col < 3 * A_W + N_W))
    q_scale = jnp.where(is_q, scale, 1.0).astype(F32)[None, :]

    cond = jnp.zeros((8, D), F32).at[:B].set(c * jax.nn.sigmoid(c)).astype(BF16)
    mods = [matmul(cond, ada_w, i, 6 * D, tm=8, tn=1024)[:B] + ada_b[i] for i in range(depth)]
    h = None
    gate = None
    for i in range(depth):
        sh1, sc1, g1, sh2, sc2, g2 = [m[:, None, :] for m in jnp.split(mods[i], 6, axis=-1)]
        if i == 0:
            h = ln_mod(x, sc1, sh1)

        hf = h.reshape(N, D)
        pa = matmul(hf, w_in_b, i, ATT_W, scale=q_scale, out_dtype=BF16, tm=2048, tn=256).reshape(B, S, ATT_W)
        pb = matmul(hf, w_in_b, i, pb_w, col0=ATT_W, tm=2048, tn=PB_TN).reshape(B, S, pb_w)
        oa = dilated_attention(pa, sl_a)
        k_cmp, v_cmp = nsa_compress(pa, cmp_wk, cmp_wv, cmp_pe, i)
        o_part, mneg = nsa_select(pa, pb, k_cmp, v_cmp, sl_n)
        on = nsa_sweep(pa, pb, o_part, mneg, sl_n)
        oc = short_conv(pb, conv_w[i])
        y = out_proj(oa.reshape(N, A_W), on.reshape(N, N_W), oc.reshape(N, C), mix_g, w_out, i).reshape(B, S, D)

        moe_layer = i % 2 == 1
        j = i // 2
        if moe_layer:
            rw = jnp.zeros((D, LANES), F32).at[:, :N_EXPERTS].set(moe_router[j])
            rb = jnp.zeros((1, LANES), F32).at[0, :N_EXPERTS].set(moe_router_b[j])
            x, h, table = res_ln(x, y, g1, ln1_g[i:i + 1], ln1_b[i:i + 1], alpha, sc=sc2, sh=sh2,
                                 router_w=rw, router_b=rb)
        else:
            x, h = res_ln(x, y, g1, ln1_g[i:i + 1], ln1_b[i:i + 1], alpha, sc=sc2, sh=sh2)

        hf = h.reshape(N, D)
        if moe_layer:
            TM = _pick(N, 1024)
            plan = moe_plan(table.reshape(N, LANES), TM)
            a = moe_gate_up(hf, moe_w_gate, moe_w_up, j, plan, TM=TM)
            ye = moe_down(a, moe_w_down, j, plan, TM=TM)
            y = moe_combine(ye, plan, N)
        else:
            a = gate_up(hf, ffn_w_gate, ffn_w_up, (j,))
            y = matmul(a, ffn_w_down, j, D, tm=2048, tn=1024, tk=1024)
        y = y.reshape(B, S, D)

        if i + 1 < depth:
            sh_n, sc_n = mods[i + 1][:, None, :D], mods[i + 1][:, None, D:2 * D]
            x, h = res_ln(x, y, g2, ln2_g[i:i + 1], ln2_b[i:i + 1], alpha, sc=sc_n, sh=sh_n)
        else:
            (x,) = res_ln(x, y, g2, ln2_g[i:i + 1], ln2_b[i:i + 1], alpha)
    return x
```

```python
import functools

import jax
import jax.numpy as jnp
from jax import lax
from jax.experimental import pallas as pl
from jax.experimental.pallas import tpu as pltpu

F32 = jnp.float32
BF16 = jnp.bfloat16

HEAD_DIM = 128
A_HEADS = 12
A_BRANCHES = ((128, 1), (512, 4), (2048, 16))
A_BLK = 128
A_SUPER = A_BLK * max(d for _, d in A_BRANCHES)
A_GROUP = 8
NSA_HEADS = 12
NSA_KV_HEADS = 3
NSA_GROUP = NSA_HEADS // NSA_KV_HEADS
CMP_LEN = 32
CMP_STRIDE = 16
SEL_BLK = 64
TOP_N = 16
NSA_WIN = 512
CONV_W = 3
N_EXPERTS = 8
LN_EPS = 1e-5
NEG = -1e30
TINY = 1e-30
FORCE = 1e4
LOWEST = -3e38

A_W = A_HEADS * HEAD_DIM
N_W = NSA_HEADS * HEAD_DIM
KV_W = NSA_KV_HEADS * HEAD_DIM
ATT_W = 3 * A_W + N_W + 6 * KV_W
GATE_W = NSA_HEADS * 3
LANES = 128
PB_TN = 2 * LANES
VMEM_LIMIT = 56 * 1024 * 1024


def _cparams(sem):
    return pltpu.CompilerParams(dimension_semantics=sem, vmem_limit_bytes=VMEM_LIMIT)


def _pick(n, pref):
    if n <= pref:
        return n
    t = pref
    while n % t:
        t //= 2
    return t


def _mm_kernel(*refs, nk, k_tail, has_scale):
    if has_scale:
        x_ref, w_ref, s_ref, o_ref, acc_ref = refs
    else:
        x_ref, w_ref, o_ref, acc_ref = refs
        s_ref = None

    def product(tail):
        x = x_ref[...]
        w = w_ref[...]
        if tail:
            x = jnp.where(lax.broadcasted_iota(jnp.int32, x.shape, 1) < tail, x, jnp.zeros_like(x))
            w = jnp.where(lax.broadcasted_iota(jnp.int32, w.shape, 0) < tail, w, jnp.zeros_like(w))
        return jnp.dot(x, w.astype(BF16), preferred_element_type=F32)

    def finish(acc):
        if s_ref is not None:
            acc = acc * s_ref[...]
        o_ref[...] = acc.astype(o_ref.dtype)

    if nk == 1:
        finish(product(k_tail))
        return
    k = pl.program_id(2)

    @pl.when(k == 0)
    def _():
        acc_ref[...] = product(0)

    @pl.when((k > 0) & (k < nk - 1))
    def _():
        acc_ref[...] += product(0)

    @pl.when(k == nk - 1)
    def _():
        finish(acc_ref[...] + product(k_tail))


def matmul(x, w, w_idx, n_cols, *, col0=0, scale=None, out_dtype=F32, tm=1024, tn=512, tk=None):
    M, K = x.shape
    tm = _pick(M, tm)
    tn = _pick(n_cols, tn)
    tk = K if tk is None or tk >= K else tk
    assert col0 % tn == 0 and col0 + n_cols <= w.shape[-1]
    nk = pl.cdiv(K, tk)
    cb = col0 // tn
    lead = (w_idx,) if w.ndim == 3 else ()
    wblk = ((None,) if w.ndim == 3 else ()) + (tk, tn)
    in_specs = [pl.BlockSpec((tm, tk), lambda i, j, k: (i, k)),
                pl.BlockSpec(wblk, lambda i, j, k: lead + (k, cb + j))]
    args = [x, w]
    if scale is not None:
        in_specs.append(pl.BlockSpec((1, tn), lambda i, j, k: (0, j)))
        args.append(scale)
    return pl.pallas_call(
        functools.partial(_mm_kernel, nk=nk, k_tail=K % tk, has_scale=scale is not None),
        out_shape=jax.ShapeDtypeStruct((M, n_cols), out_dtype),
        grid=(M // tm, n_cols // tn, nk),
        in_specs=in_specs,
        out_specs=pl.BlockSpec((tm, tn), lambda i, j, k: (i, j)),
        scratch_shapes=[pltpu.VMEM((tm, tn), F32)],
        compiler_params=_cparams(("parallel", "parallel", "arbitrary")),
        name="matmul",
    )(*args)


def _repack_kernel(wt_ref, o_ref, *, n_valid):
    tn = wt_ref.shape[0]
    col = pl.program_id(0) * tn + lax.broadcasted_iota(jnp.int32, (1, tn), 1)
    for layer in range(wt_ref.shape[1]):
        w = jnp.transpose(wt_ref[:, layer, :]).astype(BF16)
        o_ref[layer] = jnp.where(col < n_valid, w, jnp.zeros_like(w))


def repack_weight(wt, n_cols, *, tn=256):
    N, L, K = wt.shape
    assert n_cols % tn == 0 and n_cols >= N
    return pl.pallas_call(
        functools.partial(_repack_kernel, n_valid=N),
        out_shape=jax.ShapeDtypeStruct((L, K, n_cols), BF16),
        grid=(n_cols // tn,),
        in_specs=[pl.BlockSpec((tn, L, K), lambda j: (j, 0, 0))],
        out_specs=pl.BlockSpec((L, K, tn), lambda j: (0, 0, j)),
        compiler_params=_cparams(("parallel",)),
        name="repack_weight",
    )(wt)


def _gate_up_kernel(x_ref, wg_ref, wu_ref, o_ref):
    x = x_ref[...]
    g = jnp.dot(x, wg_ref[...].astype(BF16), preferred_element_type=F32)
    u = jnp.dot(x, wu_ref[...].astype(BF16), preferred_element_type=F32)
    o_ref[...] = (g * (1.0 / (1.0 + jnp.exp(-g))) * u).astype(o_ref.dtype)


def gate_up(x, wg, wu, lead, *, tm=1024, tn=256):
    M, K = x.shape
    F = wg.shape[-1]
    tm = _pick(M, tm)
    tn = _pick(F, tn)
    nlead = len(lead)
    wspec = pl.BlockSpec((None,) * nlead + (K, tn), lambda i, j: lead + (0, j))
    return pl.pallas_call(
        _gate_up_kernel,
        out_shape=jax.ShapeDtypeStruct((M, F), BF16),
        grid=(M // tm, F // tn),
        in_specs=[pl.BlockSpec((tm, K), lambda i, j: (i, 0)), wspec, wspec],
        out_specs=pl.BlockSpec((tm, tn), lambda i, j: (i, j)),
        compiler_params=_cparams(("parallel", "parallel")),
        name="gate_up",
    )(x, wg, wu)


def moe_plan(table, tile_rows):
    N = table.shape[0]
    e = jnp.concatenate([table[:, 0], table[:, 1]]).astype(jnp.int32)
    w = jnp.concatenate([table[:, 2], table[:, 3]])
    onehot = (e[:, None] == jnp.arange(N_EXPERTS)[None, :]).astype(jnp.int32)
    csum = jnp.cumsum(onehot, axis=0)
    rank = jnp.sum(csum * onehot, axis=1) - 1
    tiles = (csum[-1] + tile_rows - 1) // tile_rows
    tile_end = jnp.cumsum(tiles)
    row = (tile_end - tiles)[e] * tile_rows + rank
    T = 2 * N // tile_rows + N_EXPERTS
    tok = (jnp.arange(2 * N, dtype=jnp.int32) % N).astype(F32)
    both = jnp.zeros((T * tile_rows, 2), F32).at[row].set(jnp.stack([tok, w], axis=1))
    src = both[:, 0].astype(jnp.int32)
    row_gate = both[:, 1]
    n_used = tile_end[-1:].astype(jnp.int32)
    t_idx = jnp.minimum(jnp.arange(T), n_used[0] - 1)
    tile_expert = jnp.minimum(jnp.searchsorted(tile_end, t_idx, side="right"), N_EXPERTS - 1).astype(jnp.int32)
    return dict(src=src.reshape(T, 1, tile_rows), row_gate=row_gate.reshape(T * tile_rows, 1),
                tile_expert=tile_expert, n_used=n_used, rows=jnp.stack([row[:N], row[N:]], axis=0), T=T)


def _row_copy(src_hbm, row, dst, slot, sem):
    return pltpu.make_async_copy(src_hbm.at[pl.ds(row, 1), :], dst.at[pl.ds(slot, 1), :], sem)


def _moe_gate_up_kernel(te_ref, nu_ref, src_ref, nxt_ref, h_hbm, wg_ref, wu_ref, o_ref, xf, xb, sem, *, TM):
    t = pl.program_id(0)
    n_used = nu_ref[0]
    used = t < n_used

    def start_gather(idx_ref):
        def issue(r, carry):
            _row_copy(h_hbm, idx_ref[0, 0, r], xf, r, sem).start()
            return carry

        lax.fori_loop(0, TM, issue, 0, unroll=8)

    @pl.when(used & (pl.program_id(1) == 0))
    def _():
        @pl.when(t == 0)
        def _():
            start_gather(src_ref)

        pltpu.make_async_copy(h_hbm.at[pl.ds(0, TM), :], xf, sem).wait()
        xb[...] = xf[...].astype(BF16)

        @pl.when(t + 1 < n_used)
        def _():
            start_gather(nxt_ref)

    @pl.when(used)
    def _():
        x = xb[...]
        g = jnp.dot(x, wg_ref[...].astype(BF16), preferred_element_type=F32)
        u = jnp.dot(x, wu_ref[...].astype(BF16), preferred_element_type=F32)
        o_ref[...] = (g * (1.0 / (1.0 + jnp.exp(-g))) * u).astype(o_ref.dtype)

    @pl.when(jnp.logical_not(used))
    def _():
        o_ref[...] = jnp.zeros_like(o_ref)


def moe_gate_up(h, wg, wu, layer, plan, *, TM, tn=256):
    N, K = h.shape
    F = wg.shape[-1]
    T = plan["T"]
    tn = _pick(F, tn)
    nj = F // tn
    wspec = pl.BlockSpec((None, None, K, tn),
                         lambda t, j, te, nu: (layer, te[t], 0, jnp.where(t < nu[0], j, nj - 1)))
    grid_spec = pltpu.PrefetchScalarGridSpec(
        num_scalar_prefetch=2,
        grid=(T, nj),
        in_specs=[pl.BlockSpec((1, 1, TM), lambda t, j, te, nu: (t, 0, 0), memory_space=pltpu.SMEM),
                  pl.BlockSpec((1, 1, TM), lambda t, j, te, nu: (jnp.minimum(t + 1, T - 1), 0, 0),
                               memory_space=pltpu.SMEM),
                  pl.BlockSpec(memory_space=pl.ANY), wspec, wspec],
        out_specs=pl.BlockSpec((TM, tn), lambda t, j, te, nu: (t, j)),
        scratch_shapes=[pltpu.VMEM((TM, K), F32), pltpu.VMEM((TM, K), BF16), pltpu.SemaphoreType.DMA],
    )
    return pl.pallas_call(
        functools.partial(_moe_gate_up_kernel, TM=TM),
        out_shape=jax.ShapeDtypeStruct((T * TM, F), BF16),
        grid_spec=grid_spec,
        compiler_params=_cparams(("arbitrary", "arbitrary")),
        name="moe_gate_up",
    )(plan["tile_expert"], plan["n_used"], plan["src"], plan["src"], h, wg, wu)


def _moe_down_kernel(te_ref, nu_ref, a_ref, w_ref, rg_ref, o_ref):
    used = pl.program_id(0) < nu_ref[0]

    @pl.when(used)
    def _():
        o_ref[...] = rg_ref[...] * jnp.dot(a_ref[...], w_ref[...].astype(BF16), preferred_element_type=F32)

    @pl.when(jnp.logical_not(used))
    def _():
        o_ref[...] = jnp.zeros_like(o_ref)


def moe_down(a, wd, layer, plan, *, TM, tn=512):
    P, F = a.shape
    D = wd.shape[-1]
    T = plan["T"]
    tn = _pick(D, tn)
    nj = D // tn
    grid_spec = pltpu.PrefetchScalarGridSpec(
        num_scalar_prefetch=2,
        grid=(T, nj),
        in_specs=[pl.BlockSpec((TM, F), lambda t, j, te, nu: (t, 0)),
                  pl.BlockSpec((None, None, F, tn),
                               lambda t, j, te, nu: (layer, te[t], 0, jnp.where(t < nu[0], j, nj - 1))),
                  pl.BlockSpec((TM, 1), lambda t, j, te, nu: (t, 0))],
        out_specs=pl.BlockSpec((TM, tn), lambda t, j, te, nu: (t, j)),
    )
    return pl.pallas_call(
        _moe_down_kernel,
        out_shape=jax.ShapeDtypeStruct((P, D), F32),
        grid_spec=grid_spec,
        compiler_params=_cparams(("arbitrary", "arbitrary")),
        name="moe_down",
    )(plan["tile_expert"], plan["n_used"], a, wd, plan["row_gate"])


def _moe_combine_kernel(rows_ref, ye_hbm, o_ref, buf, sem, *, tm):
    def issue(r, carry):
        for c in range(2):
            _row_copy(ye_hbm, rows_ref[0, c, r], buf, c * tm + r, sem).start()
        return carry

    lax.fori_loop(0, tm, issue, 0, unroll=8)
    pltpu.make_async_copy(ye_hbm.at[pl.ds(0, 2 * tm), :], buf, sem).wait()
    o_ref[...] = buf[0:tm, :] + buf[tm:2 * tm, :]


def moe_combine(ye, plan, N, *, tm=256):
    D = ye.shape[1]
    tm = _pick(N, tm)
    rows = plan["rows"].reshape(2, N // tm, tm).transpose(1, 0, 2)
    return pl.pallas_call(
        functools.partial(_moe_combine_kernel, tm=tm),
        out_shape=jax.ShapeDtypeStruct((N, D), F32),
        grid=(N // tm,),
        in_specs=[pl.BlockSpec((1, 2, tm), lambda i: (i, 0, 0), memory_space=pltpu.SMEM),
                  pl.BlockSpec(memory_space=pl.ANY)],
        out_specs=pl.BlockSpec((tm, D), lambda i: (i, 0)),
        scratch_shapes=[pltpu.VMEM((2 * tm, D), F32), pltpu.SemaphoreType.DMA],
        compiler_params=_cparams(("arbitrary",)),
        name="moe_combine",
    )(rows, ye)


def _ln(v):
    mu = jnp.mean(v, axis=-1, keepdims=True)
    c = v - mu
    var = jnp.mean(c * c, axis=-1, keepdims=True)
    return c * lax.rsqrt(var + LN_EPS)


def _ln_mod_kernel(x_ref, sc_ref, sh_ref, h_ref):
    h_ref[0] = (_ln(x_ref[0]) * (1.0 + sc_ref[0]) + sh_ref[0]).astype(h_ref.dtype)


def ln_mod(x, sc, sh, *, tm=256):
    B, S, D = x.shape
    tm = _pick(S, tm)
    row = pl.BlockSpec((1, tm, D), lambda b, i: (b, i, 0))
    vec = pl.BlockSpec((1, 1, D), lambda b, i: (b, 0, 0))
    return pl.pallas_call(
        _ln_mod_kernel,
        out_shape=jax.ShapeDtypeStruct((B, S, D), BF16),
        grid=(B, S // tm),
        in_specs=[row, vec, vec],
        out_specs=row,
        compiler_params=_cparams(("parallel", "parallel")),
        name="ln_mod",
    )(x, sc, sh)


def _split3(v):
    hi = v.astype(BF16)
    r1 = v - hi.astype(F32)
    mid = r1.astype(BF16)
    lo = (r1 - mid.astype(F32)).astype(BF16)
    return hi, mid, lo


def _res_ln_kernel(*refs, alpha, with_h, with_router):
    it = iter(refs)
    x_ref, y_ref, g_ref, lng_ref, lnb_ref = (next(it) for _ in range(5))
    sc_ref = sh_ref = rw_ref = rb_ref = None
    if with_h:
        sc_ref, sh_ref = next(it), next(it)
    if with_router:
        rw_ref, rb_ref = next(it), next(it)
    xo_ref = next(it)
    h_ref = next(it) if with_h else None
    gate_ref = next(it) if with_router else None

    v = alpha * x_ref[0] + (1.0 + g_ref[0]) * y_ref[0]
    xn = _ln(v) * lng_ref[...] + lnb_ref[...]
    xo_ref[0] = xn
    if not with_h:
        return
    h = _ln(xn) * (1.0 + sc_ref[0]) + sh_ref[0]
    h_ref[0] = h.astype(h_ref.dtype)
    if not with_router:
        return
    hs = _split3(h)
    ws = _split3(rw_ref[...])
    logits = rb_ref[...]
    for a in range(3):
        for b in range(3 - a):
            logits = logits + jnp.dot(hs[a], ws[b], preferred_element_type=F32)
    lane = lax.broadcasted_iota(jnp.int32, logits.shape, 1).astype(F32)
    logits = jnp.where(lane < N_EXPERTS, logits, LOWEST)
    v1 = jnp.max(logits, axis=-1, keepdims=True)
    i1 = jnp.min(jnp.where(logits == v1, lane, float(LANES)), axis=-1, keepdims=True)
    rest = jnp.where(lane == i1, LOWEST, logits)
    v2 = jnp.max(rest, axis=-1, keepdims=True)
    i2 = jnp.min(jnp.where(rest == v2, lane, float(LANES)), axis=-1, keepdims=True)
    e2 = jnp.exp(v2 - v1)
    den = 1.0 + e2
    gate_ref[0] = jnp.where(lane == 0.0, i1, jnp.where(lane == 1.0, i2,
                            jnp.where(lane == 2.0, 1.0 / den, jnp.where(lane == 3.0, e2 / den, 0.0))))


def res_ln(x, y, g, lng, lnb, alpha, *, sc=None, sh=None, router_w=None, router_b=None, tm=256):
    B, S, D = x.shape
    tm = _pick(S, tm)
    with_h = sc is not None
    with_router = router_w is not None
    row = pl.BlockSpec((1, tm, D), lambda b, i: (b, i, 0))
    vec = pl.BlockSpec((1, 1, D), lambda b, i: (b, 0, 0))
    par = pl.BlockSpec((1, D), lambda b, i: (0, 0))
    in_specs = [row, row, vec, par, par]
    args = [x, y, g, lng, lnb]
    out_shape = [jax.ShapeDtypeStruct((B, S, D), F32)]
    out_specs = [row]
    if with_h:
        in_specs += [vec, vec]
        args += [sc, sh]
        out_shape.append(jax.ShapeDtypeStruct((B, S, D), F32 if with_router else BF16))
        out_specs.append(row)
    if with_router:
        in_specs += [pl.BlockSpec((D, LANES), lambda b, i: (0, 0)), pl.BlockSpec((1, LANES), lambda b, i: (0, 0))]
        args += [router_w, router_b]
        out_shape.append(jax.ShapeDtypeStruct((B, S, LANES), F32))
        out_specs.append(pl.BlockSpec((1, tm, LANES), lambda b, i: (b, i, 0)))
    return pl.pallas_call(
        functools.partial(_res_ln_kernel, alpha=alpha, with_h=with_h, with_router=with_router),
        out_shape=out_shape,
        grid=(B, S // tm),
        in_specs=in_specs,
        out_specs=out_specs,
        compiler_params=_cparams(("parallel", "parallel")),
        name="res_ln",
    )(*args)


def _rms(v, g):
    return (v * lax.rsqrt(jnp.mean(v * v, axis=-1, keepdims=True) + LN_EPS) * g).astype(BF16)


def _out_proj_kernel(oa_ref, on_ref, oc_ref, ga_ref, gn_ref, gc_ref, wa_ref, wn_ref, wc_ref, o_ref,
                     ya_sc, yn_sc, yc_sc):
    @pl.when(pl.program_id(1) == 0)
    def _():
        ya_sc[...] = _rms(oa_ref[...], ga_ref[...])
        yn_sc[...] = _rms(on_ref[...], gn_ref[...])
        yc_sc[...] = _rms(oc_ref[...], gc_ref[...])

    acc = jnp.dot(ya_sc[...], wa_ref[...].astype(BF16), preferred_element_type=F32)
    acc += jnp.dot(yn_sc[...], wn_ref[...].astype(BF16), preferred_element_type=F32)
    acc += jnp.dot(yc_sc[...], wc_ref[...].astype(BF16), preferred_element_type=F32)
    o_ref[...] = acc


def out_proj(oa, on, oc, mix_g, w_out, layer, *, tm=1024, tn=512):
    M = oa.shape[0]
    C = oc.shape[1]
    D = w_out.shape[-1]
    tm = _pick(M, tm)
    tn = _pick(D, tn)
    assert A_W == N_W and (A_W + N_W) % C == 0
    ga = mix_g[layer:layer + 1, :A_W]
    gn = mix_g[layer:layer + 1, A_W:A_W + N_W]
    gc = mix_g[layer:layer + 1, A_W + N_W:]
    return pl.pallas_call(
        _out_proj_kernel,
        out_shape=jax.ShapeDtypeStruct((M, D), F32),
        grid=(M // tm, D // tn),
        in_specs=[pl.BlockSpec((tm, A_W), lambda i, j: (i, 0), pipeline_mode=pl.Buffered(1)),
                  pl.BlockSpec((tm, N_W), lambda i, j: (i, 0), pipeline_mode=pl.Buffered(1)),
                  pl.BlockSpec((tm, C), lambda i, j: (i, 0), pipeline_mode=pl.Buffered(1)),
                  pl.BlockSpec((1, A_W), lambda i, j: (0, 0)),
                  pl.BlockSpec((1, N_W), lambda i, j: (0, 0)),
                  pl.BlockSpec((1, C), lambda i, j: (0, 0)),
                  pl.BlockSpec((None, A_W, tn), lambda i, j: (layer, 0, j)),
                  pl.BlockSpec((None, N_W, tn), lambda i, j: (layer, 1, j)),
                  pl.BlockSpec((None, C, tn), lambda i, j: (layer, (A_W + N_W) // C, j))],
        out_specs=pl.BlockSpec((tm, tn), lambda i, j: (i, j)),
        scratch_shapes=[pltpu.VMEM((tm, A_W), BF16), pltpu.VMEM((tm, N_W), BF16), pltpu.VMEM((tm, C), BF16)],
        compiler_params=_cparams(("parallel", "arbitrary")),
        name="out_proj",
    )(oa, on, oc, ga, gn, gc, w_out, w_out, w_out)


def _conv_kernel(cur_ref, prev_ref, w_ref, o_ref, *, C):
    i = pl.program_id(1)
    cur = cur_ref[0]
    u, bg, cg = (cur[:, GATE_W + k * C:GATE_W + (k + 1) * C] for k in range(3))
    z = cg * u
    prev = prev_ref[0]
    zp = prev[:, GATE_W + 2 * C:GATE_W + 3 * C] * prev[:, GATE_W:GATE_W + C]
    zp = jnp.where(i > 0, zp, 0.0)
    rows = lax.broadcasted_iota(jnp.int32, z.shape, 0)
    z1 = jnp.where(rows >= 1, pltpu.roll(z, 1, axis=0), zp[7:8, :])
    z2 = jnp.where(rows >= 2, pltpu.roll(z, 2, axis=0), jnp.where(rows == 1, zp[7:8, :], zp[6:7, :]))
    w = w_ref[...]
    o_ref[0] = bg * (w[0:1, :] * z2 + w[1:2, :] * z1 + w[2:3, :] * z)


def short_conv(pb, conv_w_l, *, tm=512):
    B, S, W = pb.shape
    C = conv_w_l.shape[-1]
    tm = _pick(S, tm)
    return pl.pallas_call(
        functools.partial(_conv_kernel, C=C),
        out_shape=jax.ShapeDtypeStruct((B, S, C), F32),
        grid=(B, S // tm),
        in_specs=[pl.BlockSpec((1, tm, W), lambda b, i: (b, i, 0)),
                  pl.BlockSpec((1, 8, W), lambda b, i: (b, jnp.maximum(i * (tm // 8) - 1, 0), 0)),
                  pl.BlockSpec((CONV_W, C), lambda b, i: (0, 0))],
        out_specs=pl.BlockSpec((1, tm, C), lambda b, i: (b, i, 0)),
        compiler_params=_cparams(("parallel", "parallel")),
        name="short_conv",
    )(pb, pb, conv_w_l)


def _dilated_kernel(sl_ref, q_ref, k_ref, v_ref, o_ref, qf, kf, vf, u_sc, m_sc, l_sc):
    h = pl.program_id(1)
    sb = pl.program_id(2)
    base = sb * A_SUPER

    @pl.when(sb == 0)
    def _():
        kf[...] = k_ref[0].astype(F32)
        vf[...] = v_ref[0].astype(F32)

    qf[...] = q_ref[0].astype(F32)
    slope = sl_ref[h]
    qi = lax.broadcasted_iota(jnp.int32, (A_BLK, A_BLK), 0)
    kj = lax.broadcasted_iota(jnp.int32, (A_BLK, A_BLK), 1)
    diff = (qi - kj).astype(F32)
    cur_ok = kj <= qi
    prev_band = kj >= qi
    nt = (((1,), (1,)), ((), ()))

    for br, (_, d) in enumerate(A_BRANCHES):
        span = A_BLK * d
        sd = slope * float(d)
        bias_c = sd * diff
        bias_p = sd * (diff + float(A_BLK))

        def ds(start):
            return pl.ds(start, A_BLK, stride=d) if d > 1 else pl.ds(start, A_BLK)

        for grp in range(A_SUPER // A_BLK // A_GROUP):
            blocks = []
            for i in range(grp * A_GROUP, (grp + 1) * A_GROUP):
                loc = (i // d) * span + (i % d)
                has_prev = (sb > 0) if loc < span else True
                start = base + loc
                pstart = jnp.where(has_prev, start - span, start) if loc < span else start - span
                blocks.append((ds(loc), ds(start), ds(pstart), has_prev))
            s_c, s_p, hp = [], [], []
            for lrow, rows, prow, has_prev in blocks:
                q = qf[lrow, :].astype(BF16)
                s_c.append(lax.dot_general(q, kf[rows, :].astype(BF16), nt, preferred_element_type=F32))
                s_p.append(lax.dot_general(q, kf[prow, :].astype(BF16), nt, preferred_element_type=F32))
                hp.append(jnp.full((1, 1), jnp.where(has_prev, 0.0, NEG), F32))
            s_c = jnp.where(cur_ok[None], jnp.stack(s_c) - bias_c[None], NEG)
            s_p = jnp.where(prev_band[None], jnp.stack(s_p) - bias_p[None] + jnp.stack(hp), NEG)
            m = jnp.max(jnp.maximum(s_c, s_p), axis=-1, keepdims=True)
            e_c = jnp.exp(s_c - m)
            e_p = jnp.exp(s_p - m)
            l = jnp.sum(e_c + e_p, axis=-1, keepdims=True)
            for n, (lrow, rows, prow, _) in enumerate(blocks):
                u = jnp.dot(e_c[n].astype(BF16), vf[rows, :].astype(BF16), preferred_element_type=F32)
                u += jnp.dot(e_p[n].astype(BF16), vf[prow, :].astype(BF16), preferred_element_type=F32)
                u_sc[br, lrow, :] = u
                m_sc[br, lrow, :] = jnp.broadcast_to(m[n], (A_BLK, HEAD_DIM))
                l_sc[br, lrow, :] = jnp.broadcast_to(l[n], (A_BLK, HEAD_DIM))

    nb = len(A_BRANCHES)
    mt = m_sc[0]
    for br in range(1, nb):
        mt = jnp.maximum(mt, m_sc[br])
    num = jnp.zeros((A_SUPER, HEAD_DIM), F32)
    den = jnp.zeros((A_SUPER, HEAD_DIM), F32)
    for br in range(nb):
        w = jnp.exp(m_sc[br] - mt)
        num += w * u_sc[br]
        den += w * l_sc[br]
    o_ref[0] = num / den


def dilated_attention(pa, slopes):
    B, S, _ = pa.shape
    assert S % A_SUPER == 0
    grid_spec = pltpu.PrefetchScalarGridSpec(
        num_scalar_prefetch=1,
        grid=(B, A_HEADS, S // A_SUPER),
        in_specs=[pl.BlockSpec((1, A_SUPER, HEAD_DIM), lambda b, h, s, sl: (b, s, h)),
                  pl.BlockSpec((1, S, HEAD_DIM), lambda b, h, s, sl: (b, 0, A_HEADS + h)),
                  pl.BlockSpec((1, S, HEAD_DIM), lambda b, h, s, sl: (b, 0, 2 * A_HEADS + h))],
        out_specs=pl.BlockSpec((1, A_SUPER, HEAD_DIM), lambda b, h, s, sl: (b, s, h)),
        scratch_shapes=[pltpu.VMEM((A_SUPER, HEAD_DIM), F32),
                        pltpu.VMEM((S, HEAD_DIM), F32),
                        pltpu.VMEM((S, HEAD_DIM), F32),
                        pltpu.VMEM((len(A_BRANCHES), A_SUPER, HEAD_DIM), F32),
                        pltpu.VMEM((len(A_BRANCHES), A_SUPER, HEAD_DIM), F32),
                        pltpu.VMEM((len(A_BRANCHES), A_SUPER, HEAD_DIM), F32)],
    )
    return pl.pallas_call(
        _dilated_kernel,
        out_shape=jax.ShapeDtypeStruct((B, S, A_W), F32),
        grid_spec=grid_spec,
        compiler_params=_cparams(("parallel", "parallel", "arbitrary")),
        name="dilated_attention",
    )(slopes, pa, pa, pa)


def _compress_kernel(kc_ref, vc_ref, wk_ref, wv_ref, pe_ref, ko_ref, vo_ref, xf, bsc, *, nc):
    for x_ref, w_ref, o_ref in ((kc_ref, wk_ref, ko_ref), (vc_ref, wv_ref, vo_ref)):
        xf[...] = x_ref[0].astype(F32)
        first = jnp.zeros((nc, HEAD_DIM), F32)
        second = jnp.zeros((nc, HEAD_DIM), F32)
        for j in range(CMP_STRIDE):
            xj = xf[pl.ds(j, nc, stride=CMP_STRIDE), :]
            wa = w_ref[j * HEAD_DIM:(j + 1) * HEAD_DIM, :].astype(BF16)
            wb = w_ref[(CMP_STRIDE + j) * HEAD_DIM:(CMP_STRIDE + j + 1) * HEAD_DIM, :].astype(BF16)
            first += jnp.dot((xj + pe_ref[j:j + 1, :]).astype(BF16), wa, preferred_element_type=F32)
            second += jnp.dot((xj + pe_ref[CMP_STRIDE + j:CMP_STRIDE + j + 1, :]).astype(BF16), wb,
                              preferred_element_type=F32)
        bsc[0:nc, :] = second
        bsc[nc:nc + 8, :] = jnp.zeros((8, HEAD_DIM), F32)
        out = first + bsc[1:nc + 1, :]
        row = lax.broadcasted_iota(jnp.int32, out.shape, 0)
        o_ref[0, 0] = jnp.where(row < nc - 1, out, 0.0).astype(o_ref.dtype)


def nsa_compress(pa, cmp_wk, cmp_wv, cmp_pe, layer):
    B, S, _ = pa.shape
    nc = S // CMP_STRIDE
    kc0 = (3 * A_W + N_W) // HEAD_DIM
    vc0 = kc0 + NSA_KV_HEADS
    wspec = pl.BlockSpec((None, CMP_LEN * HEAD_DIM, HEAD_DIM), lambda b, g: (layer, 0, 0))
    ospec = pl.BlockSpec((1, 1, nc, HEAD_DIM), lambda b, g: (b, g, 0, 0))
    oshape = jax.ShapeDtypeStruct((B, NSA_KV_HEADS, nc, HEAD_DIM), BF16)
    return pl.pallas_call(
        functools.partial(_compress_kernel, nc=nc),
        out_shape=[oshape, oshape],
        grid=(B, NSA_KV_HEADS),
        in_specs=[pl.BlockSpec((1, S, HEAD_DIM), lambda b, g: (b, 0, kc0 + g)),
                  pl.BlockSpec((1, S, HEAD_DIM), lambda b, g: (b, 0, vc0 + g)),
                  wspec, wspec,
                  pl.BlockSpec((None, CMP_LEN, HEAD_DIM), lambda b, g: (layer, 0, 0))],
        out_specs=[ospec, ospec],
        scratch_shapes=[pltpu.VMEM((S, HEAD_DIM), F32), pltpu.VMEM((nc + 8, HEAD_DIM), F32)],
        compiler_params=_cparams(("parallel", "parallel")),
        name="nsa_compress",
    )(pa, pa, cmp_wk, cmp_wv, cmp_pe)


def _gate_cols(gt_ref, g, M):
    sig = 1.0 / (1.0 + jnp.exp(-gt_ref[0]))
    lane = lax.broadcasted_iota(jnp.int32, sig.shape, 1)
    return [[jnp.sum(jnp.where(lane == (g * M + m) * 3 + b, sig, 0.0), axis=-1, keepdims=True)
             for b in range(3)] for m in range(M)]


def _nsa_select_kernel(sl_ref, q_ref, gt_ref, kc_ref, vc_ref, op_ref, mn_ref, ocmp_sc, imp_sc, *, TQ, nc):
    g = pl.program_id(1)
    q0 = pl.program_id(2) * TQ
    M = NSA_GROUP
    nt = (((1,), (1,)), ((), ()))

    q = q_ref[0]
    qs = jnp.concatenate([q[:, m * HEAD_DIM:(m + 1) * HEAD_DIM] for m in range(M)], axis=0)
    slope_col = jnp.concatenate([jnp.full((TQ, 1), sl_ref[g * M + m], F32) for m in range(M)], axis=0)
    tq_i = q0 + lax.broadcasted_iota(jnp.int32, (TQ, 1), 0)
    t_col = jnp.concatenate([tq_i] * M, axis=0).astype(F32)

    def softmax_rows(s, ok):
        s = jnp.where(ok, s, NEG)
        m = jnp.max(s, axis=-1, keepdims=True)
        e = jnp.where(ok, jnp.exp(s - m), 0.0)
        den = jnp.sum(e, axis=-1, keepdims=True)
        return e / jnp.maximum(den, TINY)

    def compressed(ncols):
        cidx = lax.broadcasted_iota(jnp.int32, (1, ncols), 1)
        c_end = (cidx * CMP_STRIDE + (CMP_LEN - 1)).astype(F32)
        dist = t_col - c_end
        s = lax.dot_general(qs, kc_ref[0, 0, 0:ncols, :], nt, preferred_element_type=F32) - slope_col * dist
        p_cmp = softmax_rows(s, (dist >= 0) & (cidx < nc - 1))
        ocmp_sc[...] = jnp.dot(p_cmp.astype(BF16), vc_ref[0, 0, 0:ncols, :], preferred_element_type=F32)
        psum = p_cmp[0:TQ]
        for m in range(1, M):
            psum = psum + p_cmp[m * TQ:(m + 1) * TQ]
        ci = lax.broadcasted_iota(jnp.int32, (ncols, LANES), 0)
        ni = lax.broadcasted_iota(jnp.int32, (ncols, LANES), 1)
        overlap = ((ci * CMP_STRIDE < ni * SEL_BLK + SEL_BLK) & (ci * CMP_STRIDE + CMP_LEN - 1 >= ni * SEL_BLK)
                   & (ci < nc - 1))
        overlap = jnp.where(overlap, 1.0, 0.0).astype(BF16)
        imp = jnp.zeros((TQ, LANES), F32)
        for piece in _split3(psum):
            imp = imp + jnp.dot(piece, overlap, preferred_element_type=F32)
        imp_sc[...] = imp

    n_var = 4 if nc % (4 * LANES) == 0 else 1
    part = (q0 + TQ - 1) // (nc * CMP_STRIDE // n_var)
    for v in range(n_var):
        pl.when(part == v)(functools.partial(compressed, (v + 1) * nc // n_var))
    o_cmp = ocmp_sc[...]
    imp = imp_sc[...]
    jb = lax.broadcasted_iota(jnp.int32, (TQ, LANES), 1)
    cur = tq_i // SEL_BLK
    forced = (jb == 0) | (jb == cur) | (jb == cur - 1)
    imp = jnp.where(forced, imp + FORCE, imp)
    causal_blk = jb * SEL_BLK <= tq_i
    imp = jnp.where(causal_blk, imp, NEG)

    work = imp
    memb = jnp.zeros((TQ, LANES), F32)
    jbf = jb.astype(F32)
    for _ in range(TOP_N):
        mx = jnp.max(work, axis=-1, keepdims=True)
        first = jnp.min(jnp.where(work == mx, jbf, float(LANES)), axis=-1, keepdims=True)
        pick = jbf == first
        memb = jnp.where(pick, 1.0, memb)
        work = jnp.where(pick, LOWEST, work)
    chosen = (memb > 0.5) & causal_blk
    mn_ref[0, 0] = jnp.where(chosen, 0.0, NEG).astype(mn_ref.dtype)
    gates = _gate_cols(gt_ref, g, M)
    for m in range(M):
        op_ref[0, :, m * HEAD_DIM:(m + 1) * HEAD_DIM] = gates[m][0] * o_cmp[m * TQ:(m + 1) * TQ]


def nsa_select(pa, pb, k_cmp, v_cmp, slopes, *, TQ=512):
    B, S, _ = pa.shape
    nc = k_cmp.shape[2]
    M = NSA_GROUP
    TQ = _pick(S, TQ)
    assert S // SEL_BLK <= LANES
    q0 = 3 * A_W // (M * HEAD_DIM)
    cmp_spec = pl.BlockSpec((1, 1, nc, HEAD_DIM), lambda b, g, t, sl: (b, g, 0, 0))
    grid_spec = pltpu.PrefetchScalarGridSpec(
        num_scalar_prefetch=1,
        grid=(B, NSA_KV_HEADS, S // TQ),
        in_specs=[pl.BlockSpec((1, TQ, M * HEAD_DIM), lambda b, g, t, sl: (b, t, q0 + g)),
                  pl.BlockSpec((1, TQ, LANES), lambda b, g, t, sl: (b, t, 0)),
                  cmp_spec, cmp_spec],
        out_specs=[pl.BlockSpec((1, TQ, M * HEAD_DIM), lambda b, g, t, sl: (b, t, g)),
                   pl.BlockSpec((1, 1, TQ, LANES), lambda b, g, t, sl: (b, g, t, 0))],
        scratch_shapes=[pltpu.VMEM((M * TQ, HEAD_DIM), F32), pltpu.VMEM((TQ, LANES), F32)],
    )
    return pl.pallas_call(
        functools.partial(_nsa_select_kernel, TQ=TQ, nc=nc),
        out_shape=[jax.ShapeDtypeStruct((B, S, N_W), F32),
                   jax.ShapeDtypeStruct((B, NSA_KV_HEADS, S, LANES), BF16)],
        grid_spec=grid_spec,
        compiler_params=_cparams(("parallel", "parallel", "parallel")),
        name="nsa_select",
    )(slopes, pa, pb, k_cmp, v_cmp)


def _nsa_sweep_kernel(sl_ref, q_ref, gt_ref, op_ref, mn_ref, ks_ref, vs_ref, kw_ref, vw_ref, o_ref,
                      qa_sc, mx_sc, l_sc, acc_sc, s_sc, flag_ref, act_ref, *, TQ, n_kt, G):
    g = pl.program_id(1)
    qt = pl.program_id(2)
    M = NSA_GROUP
    R = M * TQ
    kt = TQ
    bpk = kt // SEL_BLK
    nt = (((1,), (1,)), ((), ()))

    q = q_ref[0]
    mn = mn_ref[0, 0]
    for m in range(M):
        qa_sc[m * TQ:(m + 1) * TQ, 0:HEAD_DIM] = q[:, m * HEAD_DIM:(m + 1) * HEAD_DIM]
        qa_sc[m * TQ:(m + 1) * TQ, HEAD_DIM:2 * HEAD_DIM] = mn
    slope_rep = jnp.concatenate([jnp.full((TQ, kt), sl_ref[g * M + m], F32) for m in range(M)], axis=0)
    trel = jnp.concatenate([lax.broadcasted_iota(jnp.int32, (TQ, 1), 0)] * M, axis=0)
    lane = lax.broadcasted_iota(jnp.int32, (1, kt), 1)

    cnt = jnp.sum(jnp.where(mn.astype(F32) == 0.0, 1.0, 0.0), axis=0, keepdims=True)
    cnt = jnp.broadcast_to(cnt, (8, LANES))
    per_tile = cnt
    for b in range(1, bpk):
        per_tile = per_tile + pltpu.roll(cnt, LANES - b, axis=1)
    for j in range(n_kt):
        flag_ref[j] = (per_tile[0, j * bpk] > 0.0).astype(jnp.int32)

    act_ref[0] = 0

    def compact(j, n):
        act_ref[n] = j
        return n + flag_ref[j]

    n_act = lax.fori_loop(0, qt, compact, 0)
    for u in range(1, G):
        act_ref[n_act + u - 1] = act_ref[0]
    n_grp = (n_act + G - 1) // G

    def rows_of(j):
        return pl.ds(pl.multiple_of(j * kt, kt), kt)

    def sel_scores(j, shift):
        ki = lax.broadcasted_iota(jnp.int32, (kt, LANES), 0)
        bi = lax.broadcasted_iota(jnp.int32, (kt, LANES), 1)
        onehot = jnp.where(bi == j * bpk + ki // SEL_BLK, 1.0, 0.0).astype(BF16)
        s = lax.dot_general(qa_sc[...], jnp.concatenate([ks_ref[0, rows_of(j), :], onehot], axis=1), nt,
                            preferred_element_type=F32)
        return s + slope_rep * (((j - qt) * kt + lane).astype(F32) + shift)

    mx_sc[...] = jnp.full((R, kt), 0.5 * NEG, F32)

    def max_body(gi, carry):
        m = mx_sc[...]
        for u in range(G):
            idx = gi * G + u
            s = sel_scores(act_ref[idx], jnp.where(idx < n_act, 0.0, -1e34))
            s_sc[idx] = s
            m = jnp.maximum(m, s)
        mx_sc[...] = m
        return carry

    lax.fori_loop(0, n_grp, max_body, 0)
    s_diag = jnp.where(lane <= trel, sel_scores(qt, 0.0), NEG)
    m_row = jnp.max(jnp.maximum(mx_sc[...], s_diag), axis=-1, keepdims=True)
    mx_sc[...] = jnp.broadcast_to(m_row, (R, kt))

    p = jnp.exp(s_diag - m_row)
    l_sc[...] = p
    acc_sc[...] = jnp.dot(p.astype(BF16), vs_ref[0, rows_of(qt), :], preferred_element_type=F32)

    def acc_body(gi, carry):
        m = mx_sc[...]
        l = l_sc[...]
        ps, vs = [], []
        for u in range(G):
            idx = gi * G + u
            p = jnp.exp(s_sc[idx] - m)
            l = l + p
            ps.append(p.astype(BF16))
            vs.append(vs_ref[0, rows_of(act_ref[idx]), :])
        l_sc[...] = l
        acc_sc[...] += jnp.dot(jnp.concatenate(ps, axis=1), jnp.concatenate(vs, axis=0), preferred_element_type=F32)
        return carry

    lax.fori_loop(0, n_grp, acc_body, 0)
    o_sel = acc_sc[...] / jnp.maximum(jnp.sum(l_sc[...], axis=-1, keepdims=True), TINY)

    n_w = NSA_WIN // kt + 1
    s_w, v_w = [], []
    for u in range(n_w):
        jc = jnp.maximum(qt - (n_w - 1) + u, 0)
        prel = (u - (n_w - 1)) * kt + lane
        d = trel - prel
        ok = (d >= 0) & (d < NSA_WIN) & (prel >= -qt * kt)
        s = lax.dot_general(qa_sc[:, 0:HEAD_DIM], kw_ref[0, rows_of(jc), :], nt, preferred_element_type=F32)
        s_w.append(jnp.where(ok, s + slope_rep * prel.astype(F32), NEG))
        v_w.append(vw_ref[0, rows_of(jc), :])
    m_el = s_w[0]
    for s in s_w[1:]:
        m_el = jnp.maximum(m_el, s)
    m_row = jnp.max(m_el, axis=-1, keepdims=True)
    p_w = [jnp.exp(s - m_row) for s in s_w]
    l_el = p_w[0]
    for p in p_w[1:]:
        l_el = l_el + p
    o_win = jnp.dot(jnp.concatenate([p.astype(BF16) for p in p_w], axis=1), jnp.concatenate(v_w, axis=0),
                    preferred_element_type=F32) / jnp.maximum(jnp.sum(l_el, axis=-1, keepdims=True), TINY)

    gates = _gate_cols(gt_ref, g, M)
    for m in range(M):
        rs = slice(m * TQ, (m + 1) * TQ)
        cs = slice(m * HEAD_DIM, (m + 1) * HEAD_DIM)
        o_ref[0, :, cs] = op_ref[0, :, cs] + gates[m][1] * o_sel[rs] + gates[m][2] * o_win[rs]


def nsa_sweep(pa, pb, o_part, mneg, slopes, *, TQ=128, group=4):
    B, S, _ = pa.shape
    M = NSA_GROUP
    assert TQ == LANES and TQ % SEL_BLK == 0 and NSA_WIN % TQ == 0 and S % TQ == 0
    q0 = 3 * A_W // (M * HEAD_DIM)
    kv0 = (3 * A_W + N_W) // HEAD_DIM + 2 * NSA_KV_HEADS
    kv = lambda n: pl.BlockSpec((1, S, HEAD_DIM), lambda b, g, t, sl: (b, 0, kv0 + n * NSA_KV_HEADS + g))
    qspec = pl.BlockSpec((1, TQ, M * HEAD_DIM), lambda b, g, t, sl: (b, t, q0 + g))
    ospec = pl.BlockSpec((1, TQ, M * HEAD_DIM), lambda b, g, t, sl: (b, t, g))
    grid_spec = pltpu.PrefetchScalarGridSpec(
        num_scalar_prefetch=1,
        grid=(B, NSA_KV_HEADS, S // TQ),
        in_specs=[qspec,
                  pl.BlockSpec((1, TQ, LANES), lambda b, g, t, sl: (b, t, 0)),
                  ospec,
                  pl.BlockSpec((1, 1, TQ, LANES), lambda b, g, t, sl: (b, g, t, 0)),
                  kv(0), kv(1), kv(2), kv(3)],
        out_specs=ospec,
        scratch_shapes=[pltpu.VMEM((M * TQ, 2 * HEAD_DIM), BF16),
                        pltpu.VMEM((M * TQ, TQ), F32), pltpu.VMEM((M * TQ, TQ), F32),
                        pltpu.VMEM((M * TQ, HEAD_DIM), F32),
                        pltpu.VMEM((S // TQ + group, M * TQ, TQ), F32),
                        pltpu.SMEM((S // TQ,), jnp.int32),
                        pltpu.SMEM((S // TQ + group,), jnp.int32)],
    )
    return pl.pallas_call(
        functools.partial(_nsa_sweep_kernel, TQ=TQ, n_kt=S // TQ, G=group),
        out_shape=jax.ShapeDtypeStruct((B, S, N_W), F32),
        grid_spec=grid_spec,
        compiler_params=_cparams(("parallel", "parallel", "arbitrary")),
        name="nsa_sweep",
    )(slopes, pa, pb, o_part, mneg, pa, pa, pa, pa)


def kernel(x, c, ada_w, ada_b, w_in, conv_w, cmp_wk, cmp_wv, cmp_pe, mix_g, w_out, ln1_g, ln1_b, ln2_g, ln2_b,
           ffn_w_gate, ffn_w_up, ffn_w_down, moe_router, moe_router_b, moe_w_gate, moe_w_up, moe_w_down):
    B, S, D = x.shape
    depth = ada_w.shape[0]
    C = conv_w.shape[-1]
    N = B * S
    alpha = (2 * depth) ** 0.25
    assert D == A_W + N_W + C and w_in.shape[-1] == ATT_W + GATE_W + 3 * C and ATT_W % PB_TN == 0
    pb_w = pl.cdiv(GATE_W + 3 * C, PB_TN) * PB_TN
    w_in_b = repack_weight(jnp.transpose(w_in, (2, 0, 1)), ATT_W + pb_w, tn=PB_TN)

    n_heads = A_HEADS + NSA_HEADS
    sl = jnp.exp2(-8.0 * jnp.arange(1, n_heads + 1, dtype=F32) / n_heads)
    sl_a, sl_n = sl[0::2], sl[1::2]
    scale = HEAD_DIM ** -0.5
    col = jnp.arange(ATT_W)
    is_q = (col < A_W) | ((col >= 3 * A_W) & (col < 3 * A_W + N_W))
    q_scale = jnp.where(is_q, scale, 1.0).astype(F32)[None, :]

    cond = jnp.zeros((8, D), F32).at[:B].set(c * jax.nn.sigmoid(c)).astype(BF16)
    mods = [matmul(cond, ada_w, i, 6 * D, tm=8, tn=1024)[:B] + ada_b[i] for i in range(depth)]
    h = None
    gate = None
    for i in range(depth):
        sh1, sc1, g1, sh2, sc2, g2 = [m[:, None, :] for m in jnp.split(mods[i], 6, axis=-1)]
        if i == 0:
            h = ln_mod(x, sc1, sh1)

        hf = h.reshape(N, D)
        pa = matmul(hf, w_in_b, i, ATT_W, scale=q_scale, out_dtype=BF16, tm=2048, tn=256).reshape(B, S, ATT_W)
        pb = matmul(hf, w_in_b, i, pb_w, col0=ATT_W, tm=2048, tn=PB_TN).reshape(B, S, pb_w)
        oa = dilated_attention(pa, sl_a)
        k_cmp, v_cmp = nsa_compress(pa, cmp_wk, cmp_wv, cmp_pe, i)
        o_part, mneg = nsa_select(pa, pb, k_cmp, v_cmp, sl_n)
        on = nsa_sweep(pa, pb, o_part, mneg, sl_n)
        oc = short_conv(pb, conv_w[i])
        y = out_proj(oa.reshape(N, A_W), on.reshape(N, N_W), oc.reshape(N, C), mix_g, w_out, i).reshape(B, S, D)

        moe_layer = i % 2 == 1
        j = i // 2
        if moe_layer:
            rw = jnp.zeros((D, LANES), F32).at[:, :N_EXPERTS].set(moe_router[j])
            rb = jnp.zeros((1, LANES), F32).at[0, :N_EXPERTS].set(moe_router_b[j])
            x, h, table = res_ln(x, y, g1, ln1_g[i:i + 1], ln1_b[i:i + 1], alpha, sc=sc2, sh=sh2,
                                 router_w=rw, router_b=rb)
        else:
            x, h = res_ln(x, y, g1, ln1_g[i:i + 1], ln1_b[i:i + 1], alpha, sc=sc2, sh=sh2)

        hf = h.reshape(N, D)
        if moe_layer:
            TM = _pick(N, 1024)
            plan = moe_plan(table.reshape(N, LANES), TM)
            a = moe_gate_up(hf, moe_w_gate, moe_w_up, j, plan, TM=TM)
            ye = moe_down(a, moe_w_down, j, plan, TM=TM)
            y = moe_combine(ye, plan, N)
        else:
            a = gate_up(hf, ffn_w_gate, ffn_w_up, (j,))
            y = matmul(a, ffn_w_down, j, D, tm=2048, tn=1024, tk=1024)
        y = y.reshape(B, S, D)

        if i + 1 < depth:
            sh_n, sc_n = mods[i + 1][:, None, :D], mods[i + 1][:, None, D:2 * D]
            x, h = res_ln(x, y, g2, ln2_g[i:i + 1], ln2_b[i:i + 1], alpha, sc=sc_n, sh=sh_n)
        else:
            (x,) = res_ln(x, y, g2, ln2_g[i:i + 1], ln2_b[i:i + 1], alpha)
    return x
```
